```python
import jax, jax.numpy as jnp
from jax import lax
import numpy as np

D_MODEL = 1024
BATCH = 2
SEQ = 8192
DEPTH = 4
DEC_BATCH = 32
DEC_SEQ = 1
PAST_LEN = 8192
PAGE_SIZE = 128

N_MIXERS = 3
N_HEADS = 16
HEAD_DIM = D_MODEL // N_HEADS
Q_BLOCK = 128
POOL_WINDOWS = (2, 4, 8, 16)
N_POOL_GROUPS = len(POOL_WINDOWS)
POOL_GROUP = D_MODEL // N_POOL_GROUPS
POOL_HIST = max(POOL_WINDOWS) - 1
CONV_WIDTH = 31
CONV_HIST = CONV_WIDTH - 1
N_A_LAYERS = len(range(0, DEPTH, N_MIXERS))
N_B_LAYERS = len(range(1, DEPTH, N_MIXERS))
N_C_LAYERS = len(range(2, DEPTH, N_MIXERS))
RMS_EPS = 1e-6
LN_EPS = 1e-5

kernel_name = 'hybrid_stickbreak_pool_conformer_decode_step'


def rmsnorm(x, g):
    xf = x.astype(jnp.float32)
    y = xf * lax.rsqrt(jnp.mean(xf * xf, axis=-1, keepdims=True) + RMS_EPS)
    return (y * g.astype(jnp.float32)).astype(x.dtype)


def layernorm(x, g, b):
    xf = x.astype(jnp.float32)
    mu = jnp.mean(xf, axis=-1, keepdims=True)
    xc = xf - mu
    y = xc * lax.rsqrt(jnp.mean(xc * xc, axis=-1, keepdims=True) + LN_EPS)
    return (y * g.astype(jnp.float32) + b.astype(jnp.float32)).astype(x.dtype)


def stick_breaking(q, k, v, q_pos, k_pos, bias):
    z = jnp.einsum('bqhd,bkhd->bhqk', q, k).astype(jnp.float32) * (HEAD_DIM ** -0.5)
    z = z + bias.astype(jnp.float32)[None, :, None, None]
    mask = k_pos[None, :] < q_pos[:, None]
    log_fail = jnp.where(mask, jax.nn.log_sigmoid(-z), 0.0)
    log_survive = lax.cumsum(log_fail, axis=3, reverse=True) - log_fail
    a = jnp.where(mask, jnp.exp(jax.nn.log_sigmoid(z) + log_survive), 0.0)
    return jnp.einsum('bhqk,bkhd->bqhd', a.astype(v.dtype), v)


def stick_breaking_prompt(q, k, v, bias):
    b, s = q.shape[:2]
    nb = s // Q_BLOCK
    qb = q.reshape(b, nb, Q_BLOCK, N_HEADS, HEAD_DIM).transpose(1, 0, 2, 3, 4)
    k_pos = jnp.arange(s)
    q_pos = k_pos.reshape(nb, Q_BLOCK)
    ob = lax.map(lambda args: stick_breaking(args[0], k, v, args[1], k_pos, bias), (qb, q_pos))
    return ob.transpose(1, 0, 2, 3, 4).reshape(b, s, D_MODEL)


def split_attn(h, w_in):
    b, t, _ = h.shape
    q, k, v, g = jnp.split(h @ w_in, 4, axis=-1)
    hd = lambda a: a.reshape(b, t, N_HEADS, HEAD_DIM)
    return hd(q), hd(k), hd(v), g


def pool_mix(u, hist, pos0, w_group, scale):
    b, t, d = u.shape
    ext = jnp.concatenate([hist, u], axis=1)
    cs = jnp.cumsum(ext.astype(jnp.float32), axis=1)
    cs = jnp.concatenate([jnp.zeros((b, 1, d), jnp.float32), cs], axis=1)
    pos = pos0 + jnp.arange(t)
    end = cs[:, POOL_HIST + 1:POOL_HIST + 1 + t]
    means = []
    for gi, w in enumerate(POOL_WINDOWS):
        sl = slice(gi * POOL_GROUP, (gi + 1) * POOL_GROUP)
        start = cs[:, POOL_HIST + 1 - w:POOL_HIST + 1 - w + t, sl]
        cnt = jnp.minimum(w, pos + 1).astype(jnp.float32)[None, :, None]
        means.append((end[..., sl] - start) / cnt)
    mean = jnp.concatenate(means, axis=-1)
    mixed = (mean - u.astype(jnp.float32)).astype(u.dtype).reshape(b, t, N_POOL_GROUPS, POOL_GROUP)
    out = jnp.einsum('btgc,gcd->btgd', mixed, w_group).reshape(b, t, d) * scale
    return out, ext[:, -POOL_HIST:]


def conv_mix(a, hist, w_dw, b_dw, ln_g, ln_b):
    ext = jnp.concatenate([hist, a], axis=1)
    y = lax.conv_general_dilated(ext, w_dw[:, None, :], window_strides=(1,), padding='VALID',
                                 dimension_numbers=('NWC', 'WIO', 'NWC'),
                                 feature_group_count=D_MODEL)
    y = layernorm(y + b_dw, ln_g, ln_b)
    return jax.nn.silu(y), ext[:, -CONV_HIST:]


def setup_inputs(seed: int = 0) -> dict:
    key = jax.random.key(seed)
    ks = jax.random.split(key, 24)
    n_pages = PAST_LEN // PAGE_SIZE
    n_used = DEC_BATCH * n_pages
    n_pool = n_used + n_used // 4
    d = D_MODEL
    sd = d ** -0.5
    nrm = lambda k, shape, s=1.0: jax.random.normal(k, shape, jnp.float32) * s
    page_table = jax.random.permutation(ks[0], n_pool)[:n_used].reshape(DEC_BATCH, n_pages).astype(jnp.int32)
    bias0 = -float(np.log(max(SEQ, PAST_LEN + DEC_SEQ)))
    return {
        'x_prompt': nrm(ks[1], (BATCH, SEQ, d)),
        'x_sample': nrm(ks[2], (DEC_BATCH, DEC_SEQ, d)),
        'cache_k': nrm(ks[3], (N_A_LAYERS, n_pool, PAGE_SIZE, N_HEADS, HEAD_DIM)),
        'cache_v': nrm(ks[4], (N_A_LAYERS, n_pool, PAGE_SIZE, N_HEADS, HEAD_DIM)),
        'state_pool': nrm(ks[5], (N_B_LAYERS, DEC_BATCH, POOL_HIST, d)),
        'state_conv': nrm(ks[6], (N_C_LAYERS, DEC_BATCH, CONV_HIST, d), 0.5),
        'page_table': page_table,
        'norm_g': 1.0 + nrm(ks[7], (DEPTH, d), 0.02),
        'final_norm_g': 1.0 + nrm(ks[8], (d,), 0.02),
        'attn_w_in': nrm(ks[9], (N_A_LAYERS, d, 4 * d), sd),
        'attn_bias': bias0 + nrm(ks[21], (N_A_LAYERS, N_HEADS), 0.1),
        'attn_w_out': nrm(ks[10], (N_A_LAYERS, d, d), sd),
        'pool_w_in': nrm(ks[11], (N_B_LAYERS, d, 2 * d), sd),
        'pool_w_group': nrm(ks[12], (N_B_LAYERS, N_POOL_GROUPS, POOL_GROUP, POOL_GROUP), POOL_GROUP ** -0.5),
        'pool_scale': 1.0 + nrm(ks[13], (N_B_LAYERS, d), 0.1),
        'pool_w_out': nrm(ks[14], (N_B_LAYERS, d, d), sd),
        'conv_w_in': nrm(ks[15], (N_C_LAYERS, d, 3 * d), sd),
        'conv_dw_w': nrm(ks[16], (N_C_LAYERS, CONV_WIDTH, d), CONV_WIDTH ** -0.5),
        'conv_dw_b': nrm(ks[17], (N_C_LAYERS, d), 0.01),
        'conv_ln_g': 1.0 + nrm(ks[18], (N_C_LAYERS, d), 0.02),
        'conv_ln_b': nrm(ks[19], (N_C_LAYERS, d), 0.01),
        'conv_w_out': nrm(ks[20], (N_C_LAYERS, d, d), sd),
    }


def reference(x_prompt, x_sample, cache_k, cache_v, state_pool, state_conv, page_table,
              norm_g, final_norm_g, attn_w_in, attn_bias, attn_w_out, pool_w_in, pool_w_group, pool_scale,
              pool_w_out, conv_w_in, conv_dw_w, conv_dw_b, conv_ln_g, conv_ln_b, conv_w_out):
    b, s, d = x_prompt.shape
    db, ds, _ = x_sample.shape
    past = page_table.shape[1] * PAGE_SIZE
    hp, hs = x_prompt, x_sample
    kp_l, vp_l, kss_l, vss_l = [], [], [], []
    poolp_l, pools_l, convp_l, convs_l = [], [], [], []
    for i in range(DEPTH):
        kind = i % N_MIXERS
        j = i // N_MIXERS
        n_p = rmsnorm(hp, norm_g[i])
        n_s = rmsnorm(hs, norm_g[i])
        if kind == 0:
            qp, kp, vp, gp = split_attn(n_p, attn_w_in[j])
            op = stick_breaking_prompt(qp, kp, vp, attn_bias[j])
            qs, ks_, vs, gs = split_attn(n_s, attn_w_in[j])
            past_k = cache_k[j][page_table].reshape(db, past, N_HEADS, HEAD_DIM)
            past_v = cache_v[j][page_table].reshape(db, past, N_HEADS, HEAD_DIM)
            keys = jnp.concatenate([past_k, ks_], axis=1)
            vals = jnp.concatenate([past_v, vs], axis=1)
            os_ = stick_breaking(qs, keys, vals, past + jnp.arange(ds), jnp.arange(past + ds),
                                 attn_bias[j]).reshape(db, ds, d)
            hp = hp + (op * jax.nn.silu(gp)) @ attn_w_out[j]
            hs = hs + (os_ * jax.nn.silu(gs)) @ attn_w_out[j]
            kp_l.append(kp); vp_l.append(vp); kss_l.append(ks_); vss_l.append(vs)
        elif kind == 1:
            up, gp = jnp.split(n_p @ pool_w_in[j], 2, axis=-1)
            us, gs = jnp.split(n_s @ pool_w_in[j], 2, axis=-1)
            op, new_pp = pool_mix(up, jnp.zeros((b, POOL_HIST, d), up.dtype), 0, pool_w_group[j], pool_scale[j])
            os_, new_ps = pool_mix(us, state_pool[j], past, pool_w_group[j], pool_scale[j])
            hp = hp + (op * jax.nn.silu(gp)) @ pool_w_out[j]
            hs = hs + (os_ * jax.nn.silu(gs)) @ pool_w_out[j]
            poolp_l.append(new_pp); pools_l.append(new_ps)
        else:
            a1p, a2p, gp = jnp.split(n_p @ conv_w_in[j], 3, axis=-1)
            a1s, a2s, gs = jnp.split(n_s @ conv_w_in[j], 3, axis=-1)
            glu_p = a1p * jax.nn.sigmoid(a2p)
            glu_s = a1s * jax.nn.sigmoid(a2s)
            op, new_cp = conv_mix(glu_p, jnp.zeros((b, CONV_HIST, d), glu_p.dtype), conv_dw_w[j], conv_dw_b[j], conv_ln_g[j], conv_ln_b[j])
            os_, new_cs = conv_mix(glu_s, state_conv[j], conv_dw_w[j], conv_dw_b[j], conv_ln_g[j], conv_ln_b[j])
            hp = hp + (op * jax.nn.silu(gp)) @ conv_w_out[j]
            hs = hs + (os_ * jax.nn.silu(gs)) @ conv_w_out[j]
            convp_l.append(new_cp); convs_l.append(new_cs)
    y_prompt = rmsnorm(hp, final_norm_g)
    y_sample = rmsnorm(hs, final_norm_g)
    return (y_prompt, y_sample, jnp.stack(kp_l), jnp.stack(vp_l), jnp.stack(kss_l), jnp.stack(vss_l),
            jnp.stack(poolp_l), jnp.stack(pools_l), jnp.stack(convp_l), jnp.stack(convs_l))
```

```python
import functools
import math

import jax
import jax.numpy as jnp
from jax import lax
from jax.experimental import pallas as pl
from jax.experimental.pallas import tpu as pltpu

F32 = jnp.float32
BF16 = jnp.bfloat16

N_MIXERS = 3
N_HEADS = 16
PAGE_SIZE = 128
POOL_WINDOWS = (2, 4, 8, 16)
CONV_WIDTH = 31
RMS_EPS = 1e-6
LN_EPS = 1e-5
LOG2E = math.log2(math.e)

LANES = 128
HEADS_PER_BLOCK = 2
VMEM_LIMIT_BYTES = 56 * 1024 * 1024
ATTN_BLOCK = 256
DECODE_PAGES_PER_STEP = 8
POOL_PREV_ROWS = 16
CONV_PREV_ROWS = 32
CONV_ROW_CHUNK = 32


def _row_tile(rows, want):
    tm = min(rows, want)
    assert rows % tm == 0 and tm % 8 == 0, (rows, tm)
    return tm


def _params(*sem):
    return pltpu.CompilerParams(dimension_semantics=sem, vmem_limit_bytes=VMEM_LIMIT_BYTES)


def _sigmoid(x):
    return 1.0 / (1.0 + jnp.exp(-x))


def _silu(x):
    return x * _sigmoid(x)


def _softplus2(y):
    return jnp.maximum(y, 0.0) + jnp.log2(1.0 + jnp.exp2(-jnp.abs(y)))


def _rmsnorm(x, g):
    return x * lax.rsqrt(jnp.mean(x * x, axis=-1, keepdims=True) + RMS_EPS) * g


def _inproj_kernel(x_ref, g_ref, w_ref, *out_refs, mode, d, q_scale):
    n = _rmsnorm(x_ref[...], g_ref[...]).astype(BF16)

    def chunk(c):
        return jnp.dot(n, w_ref[:, c * d:(c + 1) * d], preferred_element_type=F32)

    if mode == "attn":
        qb_ref, k_ref, v_ref, kb_ref, vb_ref, gate_ref = out_refs
        qb_ref[...] = (chunk(0) * q_scale).astype(BF16)
        k = chunk(1)
        k_ref[...] = k
        kb_ref[...] = k.astype(BF16)
        v = chunk(2)
        v_ref[...] = v
        vb_ref[...] = v.astype(BF16)
        gate_ref[...] = chunk(3)
    elif mode == "pool":
        u_ref, gate_ref = out_refs
        u_ref[...] = chunk(0)
        gate_ref[...] = chunk(1)
    else:
        glu_ref, gate_ref = out_refs
        glu_ref[...] = chunk(0) * _sigmoid(chunk(1))
        gate_ref[...] = chunk(2)


def _inproj(x, norm_g, w_bf16, mode, tm_want):
    rows, d = x.shape
    n_out = w_bf16.shape[1]
    tm = _row_tile(rows, tm_want)
    row_spec = pl.BlockSpec((tm, d), lambda i: (i, 0))
    if mode == "attn":
        dtypes = (BF16, F32, F32, BF16, BF16, F32)
    else:
        dtypes = (F32, F32)
    q_scale = (d // N_HEADS) ** -0.5 * LOG2E
    return pl.pallas_call(
        functools.partial(_inproj_kernel, mode=mode, d=d, q_scale=q_scale),
        grid=(rows // tm,),
        in_specs=[row_spec,
                  pl.BlockSpec((1, d), lambda i: (0, 0)),
                  pl.BlockSpec((d, n_out), lambda i: (0, 0))],
        out_specs=[row_spec] * len(dtypes),
        out_shape=[jax.ShapeDtypeStruct((rows, d), t) for t in dtypes],
        compiler_params=_params("arbitrary"),
        name=f"inproj_{mode}",
    )(x, norm_g.reshape(1, d), w_bf16)


def _gate_out(mix, gate, h, w_out_ref):
    act = (mix * _silu(gate)).astype(BF16)
    return h + jnp.dot(act, w_out_ref[...], preferred_element_type=F32)


def _outproj_kernel(o_ref, gate_ref, h_ref, w_ref, fg_ref, out_ref, *, final_norm):
    h_new = _gate_out(o_ref[...], gate_ref[...], h_ref[...], w_ref)
    if final_norm:
        h_new = _rmsnorm(h_new, fg_ref[...])
    out_ref[...] = h_new


def _outproj(o, gate, h, w_out_bf16, final_g, final_norm, tm_want):
    rows, d = h.shape
    tm = _row_tile(rows, tm_want)
    row_spec = pl.BlockSpec((tm, d), lambda i: (i, 0))
    return pl.pallas_call(
        functools.partial(_outproj_kernel, final_norm=final_norm),
        grid=(rows // tm,),
        in_specs=[row_spec, row_spec, row_spec,
                  pl.BlockSpec((d, d), lambda i: (0, 0)),
                  pl.BlockSpec((1, d), lambda i: (0, 0))],
        out_specs=row_spec,
        out_shape=jax.ShapeDtypeStruct((rows, d), F32),
        compiler_params=_params("arbitrary"),
        name="outproj_final" if final_norm else "outproj",
    )(o, gate, h, w_out_bf16, final_g.reshape(1, d))


def _attn_prompt_kernel(bias_ref, q_ref, k_ref, v_ref, o_ref, acc_ref, carry_ref, *, blk):
    hp = pl.program_id(1)
    qi = pl.program_id(2)
    half = LANES // HEADS_PER_BLOCK
    lane = lax.broadcasted_iota(jnp.int32, (1, LANES), 1)
    first_head = lane < half

    q = q_ref[0]
    zero = jnp.zeros_like(q)
    qq = jnp.concatenate([jnp.where(first_head, q, zero), jnp.where(first_head, zero, q)], axis=0)
    row2 = lax.broadcasted_iota(jnp.int32, (2 * blk, 1), 0)
    bias = jnp.where(row2 < blk, bias_ref[2 * hp], bias_ref[2 * hp + 1])

    jj = lax.broadcasted_iota(jnp.int32, (blk, blk), 0)
    ss = lax.broadcasted_iota(jnp.int32, (blk, blk), 1)
    suffix_ones = (jj >= ss).astype(BF16)

    acc_ref[...] = jnp.zeros_like(acc_ref)
    carry_ref[...] = jnp.zeros_like(carry_ref)

    def tile(kb, diagonal):
        start = pl.multiple_of(kb * blk, blk)
        ks = k_ref[0, pl.ds(start, blk), :]
        vs = v_ref[0, pl.ds(start, blk), :]
        y = lax.dot_general(qq, ks, (((1,), (1,)), ((), ())), preferred_element_type=F32) + bias
        sp = _softplus2(y)
        if diagonal:
            r = lax.broadcasted_iota(jnp.int32, (2 * blk, blk), 0)
            c = lax.broadcasted_iota(jnp.int32, (2 * blk, blk), 1)
            visible = c < jnp.where(r < blk, r, r - blk)
            sp = jnp.where(visible, sp, 0.0)
        c_local = jnp.dot(sp.astype(BF16), suffix_ones, preferred_element_type=F32)
        a = jnp.exp2(y - c_local)
        if diagonal:
            a = jnp.where(visible, a, 0.0)
        ab = a.astype(BF16)
        a_cat = jnp.concatenate([ab[:blk], ab[blk:]], axis=1)
        vzero = jnp.zeros_like(vs)
        v_cat = jnp.concatenate([jnp.where(first_head, vs, vzero),
                                 jnp.where(first_head, vzero, vs)], axis=0)
        part = jnp.dot(a_cat, v_cat, preferred_element_type=F32)
        carry = carry_ref[...]
        decay = jnp.exp2(-carry)
        acc_ref[...] += part * jnp.where(first_head, decay[:blk], decay[blk:])
        carry_ref[...] = carry + c_local[:, 0:1]

    tile(qi, True)

    def body(i, _):
        tile(qi - 1 - i, False)
        return 0

    lax.fori_loop(0, qi, body, 0)
    o_ref[0] = acc_ref[...]


def _attn_prompt(qb, kb, vb, bias2):
    b, s, d = qb.shape
    blk = min(ATTN_BLOCK, s)
    assert s % blk == 0
    n_pairs = d // LANES
    return pl.pallas_call(
        functools.partial(_attn_prompt_kernel, blk=blk),
        grid_spec=pltpu.PrefetchScalarGridSpec(
            num_scalar_prefetch=1,
            grid=(b, n_pairs, s // blk),
            in_specs=[pl.BlockSpec((1, blk, LANES), lambda bi, hp, qi, bias: (bi, qi, hp)),
                      pl.BlockSpec((1, s, LANES), lambda bi, hp, qi, bias: (bi, 0, hp)),
                      pl.BlockSpec((1, s, LANES), lambda bi, hp, qi, bias: (bi, 0, hp))],
            out_specs=pl.BlockSpec((1, blk, LANES), lambda bi, hp, qi, bias: (bi, qi, hp)),
            scratch_shapes=[pltpu.VMEM((blk, LANES), F32), pltpu.VMEM((2 * blk, 1), F32)]),
        out_shape=jax.ShapeDtypeStruct((b, s, d), F32),
        compiler_params=_params("arbitrary", "arbitrary", "arbitrary"),
        name="attn_prompt",
    )(bias2, qb, kb, vb)


def _attn_decode_kernel(pt_ref, q_ref, bias_ref, *refs, n_pages_step, d):
    k_refs = refs[:n_pages_step]
    v_refs = refs[n_pages_step:2 * n_pages_step]
    o_ref, qmat_ref, acc_ref, carry_ref = refs[2 * n_pages_step:]
    step = pl.program_id(1)
    dh = d // N_HEADS

    chan = lax.broadcasted_iota(jnp.int32, (N_HEADS, d), 1)
    head = lax.broadcasted_iota(jnp.int32, (N_HEADS, d), 0)
    owns = (chan >= head * dh) & (chan < (head + 1) * dh)
    spread = owns.astype(BF16)

    @pl.when(step == 0)
    def _():
        qmat_ref[...] = jnp.where(owns, q_ref[0], 0.0).astype(BF16)
        acc_ref[...] = jnp.zeros_like(acc_ref)
        carry_ref[...] = jnp.zeros_like(carry_ref)

    jj = lax.broadcasted_iota(jnp.int32, (PAGE_SIZE, PAGE_SIZE), 1)
    ss = lax.broadcasted_iota(jnp.int32, (PAGE_SIZE, PAGE_SIZE), 0)
    suffix_ones = (jj >= ss).astype(BF16)
    qmat = qmat_ref[...]
    bias = bias_ref[...]

    acc = acc_ref[...]
    carry = carry_ref[...]
    for p in range(n_pages_step):
        kp = k_refs[p][...].astype(BF16)
        y = lax.dot_general(kp, qmat, (((1,), (1,)), ((), ())), preferred_element_type=F32) + bias
        sp = _softplus2(y)
        c_local = jnp.dot(suffix_ones, sp.astype(BF16), preferred_element_type=F32)
        a = jnp.exp2(y - c_local - carry)
        a_wide = jnp.dot(a.astype(BF16), spread, preferred_element_type=F32)
        acc = acc + jnp.sum(a_wide * v_refs[p][...], axis=0, keepdims=True)
        carry = carry + c_local[0:1, :]
    acc_ref[...] = acc
    carry_ref[...] = carry

    @pl.when(step == pl.num_programs(1) - 1)
    def _():
        o_ref[0] = acc


def _attn_decode(qs, cache_k, cache_v, layer, page_table, bias2):
    db, d = qs.shape
    n_pages = page_table.shape[1]
    pps = min(DECODE_PAGES_PER_STEP, n_pages)
    assert n_pages % pps == 0
    n_pool = cache_k.shape[1]
    ck = cache_k.reshape(cache_k.shape[0], n_pool, PAGE_SIZE, d)
    cv = cache_v.reshape(cache_v.shape[0], n_pool, PAGE_SIZE, d)

    def page_spec(p):
        def index(bi, st, pt):
            return (layer, pt[bi * n_pages + (n_pages - 1 - (st * pps + p))], 0, 0)
        return pl.BlockSpec((None, None, PAGE_SIZE, d), index)

    specs = [page_spec(p) for p in range(pps)]
    out = pl.pallas_call(
        functools.partial(_attn_decode_kernel, n_pages_step=pps, d=d),
        grid_spec=pltpu.PrefetchScalarGridSpec(
            num_scalar_prefetch=1,
            grid=(db, n_pages // pps),
            in_specs=[pl.BlockSpec((1, 1, d), lambda bi, st, pt: (bi, 0, 0)),
                      pl.BlockSpec((1, N_HEADS), lambda bi, st, pt: (0, 0))] + specs + specs,
            out_specs=pl.BlockSpec((1, 1, d), lambda bi, st, pt: (bi, 0, 0)),
            scratch_shapes=[pltpu.VMEM((N_HEADS, d), BF16), pltpu.VMEM((1, d), F32),
                            pltpu.VMEM((1, N_HEADS), F32)]),
        out_shape=jax.ShapeDtypeStruct((db, 1, d), F32),
        compiler_params=_params("arbitrary", "arbitrary"),
        name="attn_decode",
    )(page_table.reshape(-1), qs.reshape(db, 1, d), bias2.reshape(1, N_HEADS),
      *([ck] * pps), *([cv] * pps))
    return out.reshape(db, d)


def _pool_groups(window_sum, u_of, inv_cnt_of, wg_ref, scale):
    outs = []
    for gi, w in enumerate(POOL_WINDOWS):
        mixed = (window_sum(gi, w) * inv_cnt_of(w) - u_of(gi)).astype(BF16)
        outs.append(jnp.dot(mixed, wg_ref[gi], preferred_element_type=F32))
    return jnp.concatenate(outs, axis=-1) * scale


def _pool_prompt_kernel(u_ref, prev_ref, gate_ref, h_ref, wg_ref, scale_ref, w_ref, out_ref, ext_ref,
                        *, tm, grp):
    t = pl.program_id(1)
    prev = prev_ref[0]
    ext_ref[0:POOL_PREV_ROWS, :] = jnp.where(t > 0, prev, jnp.zeros_like(prev))
    ext_ref[POOL_PREV_ROWS:, :] = u_ref[0]
    pos = t * tm + lax.broadcasted_iota(jnp.int32, (tm, 1), 0)

    def window_sum(gi, w):
        sl = slice(gi * grp, (gi + 1) * grp)
        s = ext_ref[POOL_PREV_ROWS:POOL_PREV_ROWS + tm, sl]
        for i in range(1, w):
            s = s + ext_ref[POOL_PREV_ROWS - i:POOL_PREV_ROWS - i + tm, sl]
        return s

    def u_of(gi):
        return ext_ref[POOL_PREV_ROWS:POOL_PREV_ROWS + tm, gi * grp:(gi + 1) * grp]

    def inv_cnt_of(w):
        return 1.0 / jnp.minimum(w, pos + 1).astype(F32)

    mix = _pool_groups(window_sum, u_of, inv_cnt_of, wg_ref, scale_ref[...])
    out_ref[0] = _gate_out(mix, gate_ref[0], h_ref[0], w_ref)


def _pool_prompt(u, gate, h, wg_bf16, scale, w_out_bf16, tm_want):
    b, s, d = u.shape
    tm = _row_tile(s, tm_want)
    assert tm % POOL_PREV_ROWS == 0
    grp = d // len(POOL_WINDOWS)
    ratio = tm // POOL_PREV_ROWS
    row_spec = pl.BlockSpec((1, tm, d), lambda bi, t: (bi, t, 0))
    return pl.pallas_call(
        functools.partial(_pool_prompt_kernel, tm=tm, grp=grp),
        grid=(b, s // tm),
        in_specs=[row_spec,
                  pl.BlockSpec((1, POOL_PREV_ROWS, d), lambda bi, t: (bi, jnp.maximum(t * ratio - 1, 0), 0)),
                  row_spec, row_spec,
                  pl.BlockSpec((len(POOL_WINDOWS), grp, grp), lambda bi, t: (0, 0, 0)),
                  pl.BlockSpec((1, d), lambda bi, t: (0, 0)),
                  pl.BlockSpec((d, d), lambda bi, t: (0, 0))],
        out_specs=row_spec,
        out_shape=jax.ShapeDtypeStruct((b, s, d), F32),
        scratch_shapes=[pltpu.VMEM((tm + POOL_PREV_ROWS, d), F32)],
        compiler_params=_params("arbitrary", "arbitrary"),
        name="pool_prompt",
    )(u, u, gate, h, wg_bf16, scale.reshape(1, d), w_out_bf16)


def _pool_sample_kernel(hist_ref, u_ref, gate_ref, h_ref, wg_ref, scale_ref, w_ref, out_ref, *, grp, past):
    n_hist = hist_ref.shape[0]

    def window_sum(gi, w):
        sl = slice(gi * grp, (gi + 1) * grp)
        s = u_ref[:, sl]
        for i in range(1, w):
            s = s + hist_ref[n_hist - i, :, sl]
        return s

    def u_of(gi):
        return u_ref[:, gi * grp:(gi + 1) * grp]

    def inv_cnt_of(w):
        return 1.0 / float(min(w, past + 1))

    mix = _pool_groups(window_sum, u_of, inv_cnt_of, wg_ref, scale_ref[...])
    out_ref[...] = _gate_out(mix, gate_ref[...], h_ref[...], w_ref)


def _pool_sample(hist_t, u, gate, h, wg_bf16, scale, w_out_bf16, past):
    db, d = u.shape
    grp = d // len(POOL_WINDOWS)
    return pl.pallas_call(
        functools.partial(_pool_sample_kernel, grp=grp, past=past),
        out_shape=jax.ShapeDtypeStruct((db, d), F32),
        compiler_params=pltpu.CompilerParams(vmem_limit_bytes=VMEM_LIMIT_BYTES),
        name="pool_sample",
    )(hist_t, u, gate, h, wg_bf16, scale.reshape(1, d), w_out_bf16)


def _ln_silu(y, ln_g, ln_b):
    mu = jnp.mean(y, axis=-1, keepdims=True)
    yc = y - mu
    yn = yc * lax.rsqrt(jnp.mean(yc * yc, axis=-1, keepdims=True) + LN_EPS) * ln_g + ln_b
    return _silu(yn)


def _conv_prompt_kernel(glu_ref, prev_ref, gate_ref, h_ref, dw_ref, db_ref, lg_ref, lb_ref, w_ref,
                        out_ref, ext_ref, mix_ref, *, tm):
    t = pl.program_id(1)
    prev = prev_ref[0]
    ext_ref[0:CONV_PREV_ROWS, :] = jnp.where(t > 0, prev, jnp.zeros_like(prev))
    ext_ref[CONV_PREV_ROWS:, :] = glu_ref[0]
    lead = CONV_PREV_ROWS - (CONV_WIDTH - 1)
    for c in range(tm // CONV_ROW_CHUNK):
        r0 = c * CONV_ROW_CHUNK
        y = dw_ref[0:1, :] * ext_ref[r0 + lead:r0 + lead + CONV_ROW_CHUNK, :]
        for i in range(1, CONV_WIDTH):
            y = y + dw_ref[i:i + 1, :] * ext_ref[r0 + lead + i:r0 + lead + i + CONV_ROW_CHUNK, :]
        mix_ref[r0:r0 + CONV_ROW_CHUNK, :] = _ln_silu(y + db_ref[...], lg_ref[...], lb_ref[...])
    out_ref[0] = _gate_out(mix_ref[...], gate_ref[0], h_ref[0], w_ref)


def _conv_prompt(glu, gate, h, dw_w, dw_b, ln_g, ln_b, w_out_bf16, tm_want):
    b, s, d = glu.shape
    tm = _row_tile(s, tm_want)
    assert tm % CONV_PREV_ROWS == 0 and tm % CONV_ROW_CHUNK == 0
    ratio = tm // CONV_PREV_ROWS
    row_spec = pl.BlockSpec((1, tm, d), lambda bi, t: (bi, t, 0))
    vec_spec = pl.BlockSpec((1, d), lambda bi, t: (0, 0))
    return pl.pallas_call(
        functools.partial(_conv_prompt_kernel, tm=tm),
        grid=(b, s // tm),
        in_specs=[row_spec,
                  pl.BlockSpec((1, CONV_PREV_ROWS, d), lambda bi, t: (bi, jnp.maximum(t * ratio - 1, 0), 0)),
                  row_spec, row_spec,
                  pl.BlockSpec((CONV_WIDTH, d), lambda bi, t: (0, 0)),
                  vec_spec, vec_spec, vec_spec,
                  pl.BlockSpec((d, d), lambda bi, t: (0, 0))],
        out_specs=row_spec,
        out_shape=jax.ShapeDtypeStruct((b, s, d), F32),
        scratch_shapes=[pltpu.VMEM((tm + CONV_PREV_ROWS, d), F32), pltpu.VMEM((tm, d), F32)],
        compiler_params=_params("arbitrary", "arbitrary"),
        name="conv_prompt",
    )(glu, glu, gate, h, dw_w, dw_b.reshape(1, d), ln_g.reshape(1, d), ln_b.reshape(1, d), w_out_bf16)


def _conv_sample_kernel(hist_ref, glu_ref, gate_ref, h_ref, dw_ref, db_ref, lg_ref, lb_ref, w_ref, out_ref):
    n_hist = hist_ref.shape[0]
    y = dw_ref[n_hist:n_hist + 1, :] * glu_ref[...]
    for i in range(n_hist):
        y = y + dw_ref[i:i + 1, :] * hist_ref[i]
    mix = _ln_silu(y + db_ref[...], lg_ref[...], lb_ref[...])
    out_ref[...] = _gate_out(mix, gate_ref[...], h_ref[...], w_ref)


def _conv_sample(hist_t, glu, gate, h, dw_w, dw_b, ln_g, ln_b, w_out_bf16):
    db, d = glu.shape
    return pl.pallas_call(
        _conv_sample_kernel,
        out_shape=jax.ShapeDtypeStruct((db, d), F32),
        compiler_params=pltpu.CompilerParams(vmem_limit_bytes=VMEM_LIMIT_BYTES),
        name="conv_sample",
    )(hist_t, glu, gate, h, dw_w, dw_b.reshape(1, d), ln_g.reshape(1, d), ln_b.reshape(1, d), w_out_bf16)


def kernel(x_prompt, x_sample, cache_k, cache_v, state_pool, state_conv, page_table, norm_g, final_norm_g,
           attn_w_in, attn_bias, attn_w_out, pool_w_in, pool_w_group, pool_scale, pool_w_out, conv_w_in,
           conv_dw_w, conv_dw_b, conv_ln_g, conv_ln_b, conv_w_out):
    b, s, d = x_prompt.shape
    db, ds, _ = x_sample.shape
    assert ds == 1 and d % LANES == 0 and d // N_HEADS * HEADS_PER_BLOCK == LANES
    depth = norm_g.shape[0]
    past = page_table.shape[1] * PAGE_SIZE
    dh = d // N_HEADS
    pool_hist = state_pool.shape[2]
    conv_hist = state_conv.shape[2]
    assert pool_hist == max(POOL_WINDOWS) - 1 and conv_hist == CONV_WIDTH - 1

    hp = x_prompt.reshape(b * s, d)
    hs = x_sample.reshape(db, d)
    kp_l, vp_l, ks_l, vs_l = [], [], [], []
    poolp_l, pools_l, convp_l, convs_l = [], [], [], []
    for i in range(depth):
        kind = i % N_MIXERS
        j = i // N_MIXERS
        last = i == depth - 1
        if kind == 0:
            w_in = attn_w_in[j].astype(BF16)
            w_out = attn_w_out[j].astype(BF16)
            bias2 = attn_bias[j] * LOG2E
            qb, k, v, kb, vb, gp = _inproj(hp, norm_g[i], w_in, "attn", 256)
            qsb, ks_, vs, _, _, gs = _inproj(hs, norm_g[i], w_in, "attn", 256)
            op = _attn_prompt(qb.reshape(b, s, d), kb.reshape(b, s, d), vb.reshape(b, s, d), bias2)
            os_ = _attn_decode(qsb.astype(F32), cache_k, cache_v, j, page_table, bias2)
            hp = _outproj(op.reshape(b * s, d), gp, hp, w_out, final_norm_g, last, 512)
            hs = _outproj(os_, gs, hs, w_out, final_norm_g, last, 512)
            kp_l.append(k.reshape(b, s, N_HEADS, dh))
            vp_l.append(v.reshape(b, s, N_HEADS, dh))
            ks_l.append(ks_.reshape(db, ds, N_HEADS, dh))
            vs_l.append(vs.reshape(db, ds, N_HEADS, dh))
        elif kind == 1:
            w_in = pool_w_in[j].astype(BF16)
            w_out = pool_w_out[j].astype(BF16)
            wg = pool_w_group[j].astype(BF16)
            up, gp = _inproj(hp, norm_g[i], w_in, "pool", 512)
            us, gs = _inproj(hs, norm_g[i], w_in, "pool", 512)
            up3 = up.reshape(b, s, d)
            hp = _pool_prompt(up3, gp.reshape(b, s, d), hp.reshape(b, s, d), wg, pool_scale[j], w_out,
                              256).reshape(b * s, d)
            hist_t = jnp.transpose(state_pool[j], (1, 0, 2))
            hs = _pool_sample(hist_t, us, gs, hs, wg, pool_scale[j], w_out, past)
            if s >= pool_hist:
                poolp_l.append(up3[:, s - pool_hist:])
            else:
                poolp_l.append(jnp.concatenate([jnp.zeros((b, pool_hist - s, d), F32), up3], axis=1))
            pools_l.append(jnp.concatenate([state_pool[j][:, ds:], us.reshape(db, ds, d)], axis=1))
        else:
            w_in = conv_w_in[j].astype(BF16)
            w_out = conv_w_out[j].astype(BF16)
            glu_p, gp = _inproj(hp, norm_g[i], w_in, "conv", 512)
            glu_s, gs = _inproj(hs, norm_g[i], w_in, "conv", 512)
            glu3 = glu_p.reshape(b, s, d)
            hp = _conv_prompt(glu3, gp.reshape(b, s, d), hp.reshape(b, s, d), conv_dw_w[j], conv_dw_b[j],
                              conv_ln_g[j], conv_ln_b[j], w_out, 256).reshape(b * s, d)
            hist_t = jnp.transpose(state_conv[j], (1, 0, 2))
            hs = _conv_sample(hist_t, glu_s, gs, hs, conv_dw_w[j], conv_dw_b[j], conv_ln_g[j], conv_ln_b[j],
                              w_out)
            if s >= conv_hist:
                convp_l.append(glu3[:, s - conv_hist:])
            else:
                convp_l.append(jnp.concatenate([jnp.zeros((b, conv_hist - s, d), F32), glu3], axis=1))
            convs_l.append(jnp.concatenate([state_conv[j][:, ds:], glu_s.reshape(db, ds, d)], axis=1))
    assert depth % N_MIXERS == 1, "the final RMSNorm is fused into the attention output projection"
    return (hp.reshape(b, s, d), hs.reshape(db, ds, d), jnp.stack(kp_l), jnp.stack(vp_l), jnp.stack(ks_l),
            jnp.stack(vs_l), jnp.stack(poolp_l), jnp.stack(pools_l), jnp.stack(convp_l), jnp.stack(convs_l))
```

```python
import functools
import math

import jax
import jax.numpy as jnp
from jax import lax
from jax.experimental import pallas as pl
from jax.experimental.pallas import tpu as pltpu

F32 = jnp.float32
BF16 = jnp.bfloat16

N_MIXERS = 3
N_HEADS = 16
PAGE_SIZE = 128
POOL_WINDOWS = (2, 4, 8, 16)
CONV_WIDTH = 31
RMS_EPS = 1e-6
LN_EPS = 1e-5
LOG2E = math.log2(math.e)

LANES = 128
HEADS_PER_BLOCK = 2
VMEM_LIMIT_BYTES = 56 * 1024 * 1024
ATTN_BLOCK = 256
ATTN_GROUP = 4
DECODE_PAGES_PER_STEP = 8
POOL_PREV_ROWS = 16
CONV_PREV_ROWS = 32
CONV_ROW_CHUNK = 32

_NT = (((1,), (1,)), ((), ()))


def _row_tile(rows, want):
    tm = min(rows, want)
    assert rows % tm == 0 and tm % 8 == 0, (rows, tm)
    return tm


def _params(*sem):
    return pltpu.CompilerParams(dimension_semantics=sem, vmem_limit_bytes=VMEM_LIMIT_BYTES)


def _sigmoid(x):
    return 1.0 / (1.0 + jnp.exp(-x))


def _silu(x):
    return x * _sigmoid(x)


def _softplus2(y):
    neg_abs = lax.bitcast_convert_type(lax.bitcast_convert_type(y, jnp.int32) | jnp.int32(-2 ** 31), F32)
    return jnp.maximum(y, 0.0) + jnp.log(1.0 + jnp.exp2(neg_abs)) * LOG2E


def _rmsnorm(x, g):
    return x * lax.rsqrt(jnp.mean(x * x, axis=-1, keepdims=True) + RMS_EPS) * g


def _inproj_kernel(x_ref, g_ref, w_ref, *out_refs, mode, d, q_scale):
    n = _rmsnorm(x_ref[...], g_ref[...]).astype(BF16)

    def chunk(c):
        return jnp.dot(n, w_ref[:, c * d:(c + 1) * d], preferred_element_type=F32)

    if mode == "attn":
        qb_ref, k_ref, v_ref, gate_ref = out_refs
        qb_ref[...] = (chunk(0) * q_scale).astype(BF16)
        k_ref[...] = chunk(1)
        v_ref[...] = chunk(2)
        gate_ref[...] = chunk(3)
    elif mode == "pool":
        u_ref, gate_ref = out_refs
        u_ref[...] = chunk(0)
        gate_ref[...] = chunk(1)
    else:
        glu_ref, gate_ref = out_refs
        glu_ref[...] = chunk(0) * _sigmoid(chunk(1))
        gate_ref[...] = chunk(2)


def _q_scale(d):
    return (d // N_HEADS) ** -0.5 * LOG2E


def _inproj(x, norm_g, w_bf16, mode, tm_want):
    rows, d = x.shape
    n_out = w_bf16.shape[1]
    tm = _row_tile(rows, tm_want)
    row_spec = pl.BlockSpec((tm, d), lambda i: (i, 0))
    dtypes = (BF16, F32, F32, F32) if mode == "attn" else (F32, F32)
    return pl.pallas_call(
        functools.partial(_inproj_kernel, mode=mode, d=d, q_scale=_q_scale(d)),
        grid=(rows // tm,),
        in_specs=[row_spec,
                  pl.BlockSpec((1, d), lambda i: (0, 0)),
                  pl.BlockSpec((d, n_out), lambda i: (0, 0))],
        out_specs=[row_spec] * len(dtypes),
        out_shape=[jax.ShapeDtypeStruct((rows, d), t) for t in dtypes],
        compiler_params=_params("arbitrary"),
        name=f"inproj_{mode}",
    )(x, norm_g.reshape(1, d), w_bf16)


def _inproj_attn_prompt_kernel(x_ref, g_ref, wt_ref, wk_ref, wg_ref,
                               qt_ref, kt_ref, kb_ref, vt_ref, vtb_ref, gate_ref, *, d, q_scale):
    n = _rmsnorm(x_ref[0], g_ref[...]).astype(BF16)

    def chunk_t(c):
        return lax.dot_general(wt_ref[c * d:(c + 1) * d, :], n, _NT, preferred_element_type=F32)

    qt_ref[0, 0] = (chunk_t(0) * q_scale).astype(BF16)
    kt_ref[0] = chunk_t(1)
    kb_ref[0] = jnp.dot(n, wk_ref[...], preferred_element_type=F32).astype(BF16)
    vt = chunk_t(2)
    vt_ref[0] = vt
    vtb_ref[0, 0] = vt.astype(BF16)
    gate_ref[0] = jnp.dot(n, wg_ref[...], preferred_element_type=F32)


def _inproj_attn_prompt(x, norm_g, w_in):
    b, s, d = x.shape
    blk = min(ATTN_BLOCK, s)
    assert s % blk == 0
    nt = s // blk
    wt = jnp.transpose(w_in[:, :3 * d]).astype(BF16)
    wk = w_in[:, d:2 * d].astype(BF16)
    wg = w_in[:, 3 * d:].astype(BF16)
    row_spec = pl.BlockSpec((1, blk, d), lambda bi, t: (bi, t, 0))
    col_spec = pl.BlockSpec((1, d, blk), lambda bi, t: (bi, 0, t))
    tile_spec = pl.BlockSpec((1, 1, d, blk), lambda bi, t: (bi, t, 0, 0))
    const = lambda shape: pl.BlockSpec(shape, lambda bi, t: (0,) * len(shape))
    return pl.pallas_call(
        functools.partial(_inproj_attn_prompt_kernel, d=d, q_scale=_q_scale(d)),
        grid=(b, nt),
        in_specs=[row_spec, const((1, d)), const((3 * d, d)), const((d, d)), const((d, d))],
        out_specs=[tile_spec, col_spec, row_spec, col_spec, tile_spec, row_spec],
        out_shape=[jax.ShapeDtypeStruct((b, nt, d, blk), BF16),
                   jax.ShapeDtypeStruct((b, d, s), F32),
                   jax.ShapeDtypeStruct((b, s, d), BF16),
                   jax.ShapeDtypeStruct((b, d, s), F32),
                   jax.ShapeDtypeStruct((b, nt, d, blk), BF16),
                   jax.ShapeDtypeStruct((b, s, d), F32)],
        compiler_params=_params("arbitrary", "arbitrary"),
        name="inproj_attn_prompt",
    )(x, norm_g.reshape(1, d), wt, wk, wg)


def _gate_out(mix, gate, h, w_out_ref):
    act = (mix * _silu(gate)).astype(BF16)
    return h + jnp.dot(act, w_out_ref[...], preferred_element_type=F32)


def _outproj_kernel(o_ref, gate_ref, h_ref, w_ref, fg_ref, out_ref, *, final_norm):
    h_new = _gate_out(o_ref[...], gate_ref[...], h_ref[...], w_ref)
    if final_norm:
        h_new = _rmsnorm(h_new, fg_ref[...])
    out_ref[...] = h_new


def _outproj(o, gate, h, w_out_bf16, final_g, final_norm, tm_want):
    rows, d = h.shape
    tm = _row_tile(rows, tm_want)
    row_spec = pl.BlockSpec((tm, d), lambda i: (i, 0))
    return pl.pallas_call(
        functools.partial(_outproj_kernel, final_norm=final_norm),
        grid=(rows // tm,),
        in_specs=[row_spec, row_spec, row_spec,
                  pl.BlockSpec((d, d), lambda i: (0, 0)),
                  pl.BlockSpec((1, d), lambda i: (0, 0))],
        out_specs=row_spec,
        out_shape=jax.ShapeDtypeStruct((rows, d), F32),
        compiler_params=_params("arbitrary"),
        name="outproj_final" if final_norm else "outproj",
    )(o, gate, h, w_out_bf16, final_g.reshape(1, d))


def _attn_prompt_kernel(bias_ref, qt_ref, k_ref, vt_ref, o_ref, acc_ref, carry_ref, *, blk):
    hp = pl.program_id(1)
    qi = pl.program_id(2)
    half = LANES // HEADS_PER_BLOCK
    chan = lax.broadcasted_iota(jnp.int32, (LANES, 1), 0)
    first_head = chan < half

    qt = qt_ref[0, 0]
    zero = jnp.zeros_like(qt)
    qq = jnp.concatenate([jnp.where(first_head, qt, zero), jnp.where(first_head, zero, qt)], axis=1)
    col2 = lax.broadcasted_iota(jnp.int32, (1, 2 * blk), 1)
    bias = jnp.where(col2 < blk, bias_ref[2 * hp], bias_ref[2 * hp + 1])

    ss = lax.broadcasted_iota(jnp.int32, (blk, blk), 0)
    jj = lax.broadcasted_iota(jnp.int32, (blk, blk), 1)
    suffix_ones = (jj >= ss).astype(BF16)

    acc_ref[...] = jnp.zeros_like(acc_ref)
    carry_ref[...] = jnp.zeros_like(carry_ref)

    def tiles(kbs, diagonal):
        n = len(kbs)
        ys = []
        for kb in kbs:
            start = kb * blk if isinstance(kb, int) else pl.multiple_of(kb * blk, blk)
            ks = k_ref[0, pl.ds(start, blk), :]
            ys.append(jnp.dot(ks, qq, preferred_element_type=F32) + bias)
        visible = None
        if any(diagonal):
            r = lax.broadcasted_iota(jnp.int32, (blk, 2 * blk), 0)
            c = lax.broadcasted_iota(jnp.int32, (blk, 2 * blk), 1)
            visible = r < jnp.where(c < blk, c, c - blk)
        cs = []
        for i in range(n):
            sp = _softplus2(ys[i])
            if diagonal[i]:
                sp = jnp.where(visible, sp, 0.0)
            cs.append(jnp.dot(suffix_ones, sp.astype(BF16), preferred_element_type=F32))
        parts = []
        for i in range(n):
            a = jnp.exp2(ys[i] - cs[i])
            if diagonal[i]:
                a = jnp.where(visible, a, 0.0)
            ab = a.astype(BF16)
            vt = vt_ref[0, kbs[i]]
            p0 = jnp.dot(vt[:half], ab[:, :blk], preferred_element_type=F32)
            p1 = jnp.dot(vt[half:], ab[:, blk:], preferred_element_type=F32)
            parts.append((p0, p1, cs[i][0:1, :]))
        return parts

    def accumulate(parts):
        for p0, p1, total in parts:
            carry = carry_ref[...]
            decay = jnp.exp2(-carry)
            acc_ref[0:half, :] += p0 * decay[:, :blk]
            acc_ref[half:, :] += p1 * decay[:, blk:]
            carry_ref[...] = carry + total

    for lead in range(ATTN_GROUP):
        @pl.when(qi % ATTN_GROUP == lead)
        def _(lead=lead):
            accumulate(tiles([qi - i for i in range(lead + 1)], [True] + [False] * lead))

    def group(g, _):
        top = qi - qi % ATTN_GROUP - 1 - ATTN_GROUP * g
        accumulate(tiles([top - i for i in range(ATTN_GROUP)], [False] * ATTN_GROUP))
        return 0

    lax.fori_loop(0, qi // ATTN_GROUP, group, 0)

    o_ref[0] = jnp.transpose(acc_ref[...])


def _attn_prompt(qt, kb, vtb, bias2):
    b, nt, d, blk = qt.shape
    s = nt * blk
    n_pairs = d // LANES
    return pl.pallas_call(
        functools.partial(_attn_prompt_kernel, blk=blk),
        grid_spec=pltpu.PrefetchScalarGridSpec(
            num_scalar_prefetch=1,
            grid=(b, n_pairs, nt),
            in_specs=[pl.BlockSpec((1, 1, LANES, blk), lambda bi, hp, qi, bias: (bi, qi, hp, 0)),
                      pl.BlockSpec((1, s, LANES), lambda bi, hp, qi, bias: (bi, 0, hp)),
                      pl.BlockSpec((1, nt, LANES, blk), lambda bi, hp, qi, bias: (bi, 0, hp, 0))],
            out_specs=pl.BlockSpec((1, blk, LANES), lambda bi, hp, qi, bias: (bi, qi, hp)),
            scratch_shapes=[pltpu.VMEM((LANES, blk), F32), pltpu.VMEM((1, 2 * blk), F32)]),
        out_shape=jax.ShapeDtypeStruct((b, s, d), F32),
        compiler_params=_params("arbitrary", "arbitrary", "arbitrary"),
        name="attn_prompt",
    )(bias2, qt, kb, vtb)


def _attn_decode_kernel(pt_ref, q_ref, bias_ref, *refs, n_pages_step):
    k_refs = refs[:n_pages_step]
    v_refs = refs[n_pages_step:2 * n_pages_step]
    o_ref, qb_ref, acc_ref, carry_ref, y_ref, a_ref = refs[2 * n_pages_step:]
    step = pl.program_id(1)

    @pl.when(step == 0)
    def _():
        qb_ref[...] = jnp.broadcast_to(q_ref[0], qb_ref.shape)
        acc_ref[...] = jnp.zeros_like(acc_ref)
        carry_ref[...] = jnp.zeros_like(carry_ref)

    jj = lax.broadcasted_iota(jnp.int32, (PAGE_SIZE, PAGE_SIZE), 0)
    ss = lax.broadcasted_iota(jnp.int32, (PAGE_SIZE, PAGE_SIZE), 1)
    suffix_ones = (jj >= ss).astype(BF16)
    bias = bias_ref[...]
    carry = carry_ref[...]

    for p in range(n_pages_step):
        for h in range(N_HEADS):
            y_ref[h:h + 1, :] = jnp.sum(qb_ref[h] * k_refs[p][h], axis=0, keepdims=True)
        y = y_ref[...] + bias
        sp = _softplus2(y)
        c_local = jnp.dot(sp.astype(BF16), suffix_ones, preferred_element_type=F32)
        a_ref[p] = jnp.exp2(y - c_local - carry)
        carry = carry + c_local[:, 0:1]
    carry_ref[...] = carry

    for h in range(N_HEADS):
        acc = acc_ref[h]
        for p in range(n_pages_step):
            acc = acc + a_ref[p, h:h + 1, :] * v_refs[p][h]
        acc_ref[h] = acc

    @pl.when(step == pl.num_programs(1) - 1)
    def _():
        o_ref[0] = jnp.sum(acc_ref[...], axis=2, keepdims=True)


def _attn_decode(qs, cache_kt, cache_vt, layer, page_table, bias2):
    db, d = qs.shape
    dh = d // N_HEADS
    n_pages = page_table.shape[1]
    pps = min(DECODE_PAGES_PER_STEP, n_pages)
    assert n_pages % pps == 0

    def page_spec(p):
        def index(bi, st, pt):
            return (layer, pt[bi * n_pages + (n_pages - 1 - (st * pps + p))], 0, 0, 0)
        return pl.BlockSpec((None, None, N_HEADS, dh, PAGE_SIZE), index)

    specs = [page_spec(p) for p in range(pps)]
    out = pl.pallas_call(
        functools.partial(_attn_decode_kernel, n_pages_step=pps),
        grid_spec=pltpu.PrefetchScalarGridSpec(
            num_scalar_prefetch=1,
            grid=(db, n_pages // pps),
            in_specs=[pl.BlockSpec((1, N_HEADS, dh, 1), lambda bi, st, pt: (bi, 0, 0, 0)),
                      pl.BlockSpec((N_HEADS, 1), lambda bi, st, pt: (0, 0))] + specs + specs,
            out_specs=pl.BlockSpec((1, N_HEADS, dh, 1), lambda bi, st, pt: (bi, 0, 0, 0)),
            scratch_shapes=[pltpu.VMEM((N_HEADS, dh, PAGE_SIZE), F32),
                            pltpu.VMEM((N_HEADS, dh, PAGE_SIZE), F32),
                            pltpu.VMEM((N_HEADS, 1), F32),
                            pltpu.VMEM((N_HEADS, PAGE_SIZE), F32),
                            pltpu.VMEM((pps, N_HEADS, PAGE_SIZE), F32)]),
        out_shape=jax.ShapeDtypeStruct((db, N_HEADS, dh, 1), F32),
        compiler_params=_params("arbitrary", "arbitrary"),
        name="attn_decode",
    )(page_table.reshape(-1), qs.reshape(db, N_HEADS, dh, 1), bias2.reshape(N_HEADS, 1),
      *([cache_kt] * pps), *([cache_vt] * pps))
    return out.reshape(db, d)


def _pool_groups(window_sum, u_of, inv_cnt_of, wg_ref, scale):
    outs = []
    for gi, w in enumerate(POOL_WINDOWS):
        mixed = (window_sum(gi, w) * inv_cnt_of(w) - u_of(gi)).astype(BF16)
        outs.append(jnp.dot(mixed, wg_ref[gi], preferred_element_type=F32))
    return jnp.concatenate(outs, axis=-1) * scale


def _pool_prompt_kernel(u_ref, prev_ref, gate_ref, h_ref, wg_ref, scale_ref, w_ref, out_ref, ext_ref,
                        *, tm, grp):
    t = pl.program_id(1)
    prev = prev_ref[0]
    ext_ref[0:POOL_PREV_ROWS, :] = jnp.where(t > 0, prev, jnp.zeros_like(prev))
    ext_ref[POOL_PREV_ROWS:, :] = u_ref[0]
    pos = t * tm + lax.broadcasted_iota(jnp.int32, (tm, 1), 0)

    def window_sum(gi, w):
        sl = slice(gi * grp, (gi + 1) * grp)
        s = ext_ref[POOL_PREV_ROWS:POOL_PREV_ROWS + tm, sl]
        for i in range(1, w):
            s = s + ext_ref[POOL_PREV_ROWS - i:POOL_PREV_ROWS - i + tm, sl]
        return s

    def u_of(gi):
        return ext_ref[POOL_PREV_ROWS:POOL_PREV_ROWS + tm, gi * grp:(gi + 1) * grp]

    def inv_cnt_of(w):
        return 1.0 / jnp.minimum(w, pos + 1).astype(F32)

    mix = _pool_groups(window_sum, u_of, inv_cnt_of, wg_ref, scale_ref[...])
    out_ref[0] = _gate_out(mix, gate_ref[0], h_ref[0], w_ref)


def _pool_prompt(u, gate, h, wg_bf16, scale, w_out_bf16, tm_want):
    b, s, d = u.shape
    tm = _row_tile(s, tm_want)
    assert tm % POOL_PREV_ROWS == 0
    grp = d // len(POOL_WINDOWS)
    ratio = tm // POOL_PREV_ROWS
    row_spec = pl.BlockSpec((1, tm, d), lambda bi, t: (bi, t, 0))
    return pl.pallas_call(
        functools.partial(_pool_prompt_kernel, tm=tm, grp=grp),
        grid=(b, s // tm),
        in_specs=[row_spec,
                  pl.BlockSpec((1, POOL_PREV_ROWS, d), lambda bi, t: (bi, jnp.maximum(t * ratio - 1, 0), 0)),
                  row_spec, row_spec,
                  pl.BlockSpec((len(POOL_WINDOWS), grp, grp), lambda bi, t: (0, 0, 0)),
                  pl.BlockSpec((1, d), lambda bi, t: (0, 0)),
                  pl.BlockSpec((d, d), lambda bi, t: (0, 0))],
        out_specs=row_spec,
        out_shape=jax.ShapeDtypeStruct((b, s, d), F32),
        scratch_shapes=[pltpu.VMEM((tm + POOL_PREV_ROWS, d), F32)],
        compiler_params=_params("arbitrary", "arbitrary"),
        name="pool_prompt",
    )(u, u, gate, h, wg_bf16, scale.reshape(1, d), w_out_bf16)


def _pool_sample_kernel(hist_ref, u_ref, gate_ref, h_ref, wg_ref, scale_ref, w_ref, out_ref, *, grp, past):
    n_hist = hist_ref.shape[0]

    def window_sum(gi, w):
        sl = slice(gi * grp, (gi + 1) * grp)
        s = u_ref[:, sl]
        for i in range(1, w):
            s = s + hist_ref[n_hist - i, :, sl]
        return s

    def u_of(gi):
        return u_ref[:, gi * grp:(gi + 1) * grp]

    def inv_cnt_of(w):
        return 1.0 / float(min(w, past + 1))

    mix = _pool_groups(window_sum, u_of, inv_cnt_of, wg_ref, scale_ref[...])
    out_ref[...] = _gate_out(mix, gate_ref[...], h_ref[...], w_ref)


def _pool_sample(hist_t, u, gate, h, wg_bf16, scale, w_out_bf16, past):
    db, d = u.shape
    grp = d // len(POOL_WINDOWS)
    return pl.pallas_call(
        functools.partial(_pool_sample_kernel, grp=grp, past=past),
        out_shape=jax.ShapeDtypeStruct((db, d), F32),
        compiler_params=pltpu.CompilerParams(vmem_limit_bytes=VMEM_LIMIT_BYTES),
        name="pool_sample",
    )(hist_t, u, gate, h, wg_bf16, scale.reshape(1, d), w_out_bf16)


def _ln_silu(y, ln_g, ln_b):
    mu = jnp.mean(y, axis=-1, keepdims=True)
    yc = y - mu
    yn = yc * lax.rsqrt(jnp.mean(yc * yc, axis=-1, keepdims=True) + LN_EPS) * ln_g + ln_b
    return _silu(yn)


def _conv_prompt_kernel(glu_ref, prev_ref, gate_ref, h_ref, dw_ref, db_ref, lg_ref, lb_ref, w_ref,
                        out_ref, ext_ref, mix_ref, *, tm):
    t = pl.program_id(1)
    prev = prev_ref[0]
    ext_ref[0:CONV_PREV_ROWS, :] = jnp.where(t > 0, prev, jnp.zeros_like(prev))
    ext_ref[CONV_PREV_ROWS:, :] = glu_ref[0]
    lead = CONV_PREV_ROWS - (CONV_WIDTH - 1)
    for c in range(tm // CONV_ROW_CHUNK):
        r0 = c * CONV_ROW_CHUNK
        y = dw_ref[0:1, :] * ext_ref[r0 + lead:r0 + lead + CONV_ROW_CHUNK, :]
        for i in range(1, CONV_WIDTH):
            y = y + dw_ref[i:i + 1, :] * ext_ref[r0 + lead + i:r0 + lead + i + CONV_ROW_CHUNK, :]
        mix_ref[r0:r0 + CONV_ROW_CHUNK, :] = _ln_silu(y + db_ref[...], lg_ref[...], lb_ref[...])
    out_ref[0] = _gate_out(mix_ref[...], gate_ref[0], h_ref[0], w_ref)


def _conv_prompt(glu, gate, h, dw_w, dw_b, ln_g, ln_b, w_out_bf16, tm_want):
    b, s, d = glu.shape
    tm = _row_tile(s, tm_want)
    assert tm % CONV_PREV_ROWS == 0 and tm % CONV_ROW_CHUNK == 0
    ratio = tm // CONV_PREV_ROWS
    row_spec = pl.BlockSpec((1, tm, d), lambda bi, t: (bi, t, 0))
    vec_spec = pl.BlockSpec((1, d), lambda bi, t: (0, 0))
    return pl.pallas_call(
        functools.partial(_conv_prompt_kernel, tm=tm),
        grid=(b, s // tm),
        in_specs=[row_spec,
                  pl.BlockSpec((1, CONV_PREV_ROWS, d), lambda bi, t: (bi, jnp.maximum(t * ratio - 1, 0), 0)),
                  row_spec, row_spec,
                  pl.BlockSpec((CONV_WIDTH, d), lambda bi, t: (0, 0)),
                  vec_spec, vec_spec, vec_spec,
                  pl.BlockSpec((d, d), lambda bi, t: (0, 0))],
        out_specs=row_spec,
        out_shape=jax.ShapeDtypeStruct((b, s, d), F32),
        scratch_shapes=[pltpu.VMEM((tm + CONV_PREV_ROWS, d), F32), pltpu.VMEM((tm, d), F32)],
        compiler_params=_params("arbitrary", "arbitrary"),
        name="conv_prompt",
    )(glu, glu, gate, h, dw_w, dw_b.reshape(1, d), ln_g.reshape(1, d), ln_b.reshape(1, d), w_out_bf16)


def _conv_sample_kernel(hist_ref, glu_ref, gate_ref, h_ref, dw_ref, db_ref, lg_ref, lb_ref, w_ref, out_ref):
    n_hist = hist_ref.shape[0]
    y = dw_ref[n_hist:n_hist + 1, :] * glu_ref[...]
    for i in range(n_hist):
        y = y + dw_ref[i:i + 1, :] * hist_ref[i]
    mix = _ln_silu(y + db_ref[...], lg_ref[...], lb_ref[...])
    out_ref[...] = _gate_out(mix, gate_ref[...], h_ref[...], w_ref)


def _conv_sample(hist_t, glu, gate, h, dw_w, dw_b, ln_g, ln_b, w_out_bf16):
    db, d = glu.shape
    return pl.pallas_call(
        _conv_sample_kernel,
        out_shape=jax.ShapeDtypeStruct((db, d), F32),
        compiler_params=pltpu.CompilerParams(vmem_limit_bytes=VMEM_LIMIT_BYTES),
        name="conv_sample",
    )(hist_t, glu, gate, h, dw_w, dw_b.reshape(1, d), ln_g.reshape(1, d), ln_b.reshape(1, d), w_out_bf16)


def kernel(x_prompt, x_sample, cache_k, cache_v, state_pool, state_conv, page_table, norm_g, final_norm_g,
           attn_w_in, attn_bias, attn_w_out, pool_w_in, pool_w_group, pool_scale, pool_w_out, conv_w_in,
           conv_dw_w, conv_dw_b, conv_ln_g, conv_ln_b, conv_w_out):
    b, s, d = x_prompt.shape
    db, ds, _ = x_sample.shape
    assert ds == 1 and d % LANES == 0 and d // N_HEADS * HEADS_PER_BLOCK == LANES
    depth = norm_g.shape[0]
    past = page_table.shape[1] * PAGE_SIZE
    dh = d // N_HEADS
    pool_hist = state_pool.shape[2]
    conv_hist = state_conv.shape[2]
    assert pool_hist == max(POOL_WINDOWS) - 1 and conv_hist == CONV_WIDTH - 1

    cache_kt = jnp.transpose(cache_k, (0, 1, 3, 4, 2))
    cache_vt = jnp.transpose(cache_v, (0, 1, 3, 4, 2))

    def heads_last(xt):
        return jnp.transpose(xt.reshape(b, N_HEADS, dh, s), (0, 3, 1, 2))

    hp = x_prompt
    hs = x_sample.reshape(db, d)
    kp_l, vp_l, ks_l, vs_l = [], [], [], []
    poolp_l, pools_l, convp_l, convs_l = [], [], [], []
    for i in range(depth):
        kind = i % N_MIXERS
        j = i // N_MIXERS
        last = i == depth - 1
        if kind == 0:
            w_out = attn_w_out[j].astype(BF16)
            bias2 = attn_bias[j] * LOG2E
            qt, kt, kb, vt, vtb, gp = _inproj_attn_prompt(hp, norm_g[i], attn_w_in[j])
            qsb, ks_, vs, gs = _inproj(hs, norm_g[i], attn_w_in[j].astype(BF16), "attn", 256)
            op = _attn_prompt(qt, kb, vtb, bias2)
            os_ = _attn_decode(qsb.astype(F32), cache_kt, cache_vt, j, page_table, bias2)
            hp = _outproj(op.reshape(b * s, d), gp.reshape(b * s, d), hp.reshape(b * s, d), w_out,
                          final_norm_g, last, 512).reshape(b, s, d)
            hs = _outproj(os_, gs, hs, w_out, final_norm_g, last, 512)
            kp_l.append(heads_last(kt))
            vp_l.append(heads_last(vt))
            ks_l.append(ks_.reshape(db, ds, N_HEADS, dh))
            vs_l.append(vs.reshape(db, ds, N_HEADS, dh))
        elif kind == 1:
            w_in = pool_w_in[j].astype(BF16)
            w_out = pool_w_out[j].astype(BF16)
            wg = pool_w_group[j].astype(BF16)
            up, gp = _inproj(hp.reshape(b * s, d), norm_g[i], w_in, "pool", 512)
            us, gs = _inproj(hs, norm_g[i], w_in, "pool", 512)
            up3 = up.reshape(b, s, d)
            hp = _pool_prompt(up3, gp.reshape(b, s, d), hp, wg, pool_scale[j], w_out, 256)
            hist_t = jnp.transpose(state_pool[j], (1, 0, 2))
            hs = _pool_sample(hist_t, us, gs, hs, wg, pool_scale[j], w_out, past)
            if s >= pool_hist:
                poolp_l.append(up3[:, s - pool_hist:])
            else:
                poolp_l.append(jnp.concatenate([jnp.zeros((b, pool_hist - s, d), F32), up3], axis=1))
            pools_l.append(jnp.concatenate([state_pool[j][:, ds:], us.reshape(db, ds, d)], axis=1))
        else:
            w_in = conv_w_in[j].astype(BF16)
            w_out = conv_w_out[j].astype(BF16)
            glu_p, gp = _inproj(hp.reshape(b * s, d), norm_g[i], w_in, "conv", 512)
            glu_s, gs = _inproj(hs, norm_g[i], w_in, "conv", 512)
            glu3 = glu_p.reshape(b, s, d)
            hp = _conv_prompt(glu3, gp.reshape(b, s, d), hp, conv_dw_w[j], conv_dw_b[j],
                              conv_ln_g[j], conv_ln_b[j], w_out, 256)
            hist_t = jnp.transpose(state_conv[j], (1, 0, 2))
            hs = _conv_sample(hist_t, glu_s, gs, hs, conv_dw_w[j], conv_dw_b[j], conv_ln_g[j], conv_ln_b[j],
                              w_out)
            if s >= conv_hist:
                convp_l.append(glu3[:, s - conv_hist:])
            else:
                convp_l.append(jnp.concatenate([jnp.zeros((b, conv_hist - s, d), F32), glu3], axis=1))
            convs_l.append(jnp.concatenate([state_conv[j][:, ds:], glu_s.reshape(db, ds, d)], axis=1))
    assert depth % N_MIXERS == 1, "the final RMSNorm is fused into the attention output projection"
    return (hp, hs.reshape(db, ds, d), jnp.stack(kp_l), jnp.stack(vp_l), jnp.stack(ks_l),
            jnp.stack(vs_l), jnp.stack(poolp_l), jnp.stack(pools_l), jnp.stack(convp_l), jnp.stack(convs_l))
```

```python
import functools
import math

import jax
import jax.numpy as jnp
from jax import lax
from jax.experimental import pallas as pl
from jax.experimental.pallas import tpu as pltpu

F32 = jnp.float32
BF16 = jnp.bfloat16

N_MIXERS = 3
N_HEADS = 16
PAGE_SIZE = 128
POOL_WINDOWS = (2, 4, 8, 16)
CONV_WIDTH = 31
RMS_EPS = 1e-6
LN_EPS = 1e-5
LOG2E = math.log2(math.e)

LANES = 128
SUBLANES = 8
HEADS_PER_BLOCK = 2
VMEM_LIMIT_BYTES = 56 * 1024 * 1024
ATTN_BLOCK = 256
ATTN_GROUP = 6
DECODE_PAGES_PER_STEP = 8
POOL_PREV_ROWS = 16
CONV_PREV_ROWS = 32
CONV_ROW_CHUNK = 32

_NT = (((1,), (1,)), ((), ()))


def _row_tile(rows, want):
    tm = min(rows, want)
    assert rows % tm == 0 and tm % 8 == 0, (rows, tm)
    return tm


def _params(*sem):
    return pltpu.CompilerParams(dimension_semantics=sem, vmem_limit_bytes=VMEM_LIMIT_BYTES)


def _sigmoid(x):
    return 1.0 / (1.0 + jnp.exp(-x))


def _silu(x):
    return x * _sigmoid(x)


def _softplus2(y):
    neg_abs = lax.bitcast_convert_type(lax.bitcast_convert_type(y, jnp.int32) | jnp.int32(-2 ** 31), F32)
    return jnp.maximum(y, 0.0) + jnp.log(1.0 + jnp.exp2(neg_abs)) * LOG2E


def _rmsnorm(x, g):
    return x * lax.rsqrt(jnp.mean(x * x, axis=-1, keepdims=True) + RMS_EPS) * g


def _inproj_kernel(x_ref, g_ref, w_ref, *out_refs, mode, d, q_scale):
    n = _rmsnorm(x_ref[...], g_ref[...]).astype(BF16)

    def chunk(c):
        return jnp.dot(n, w_ref[:, c * d:(c + 1) * d], preferred_element_type=F32)

    if mode == "attn":
        qb_ref, k_ref, v_ref, gate_ref = out_refs
        qb_ref[...] = (chunk(0) * q_scale).astype(BF16)
        k_ref[...] = chunk(1)
        v_ref[...] = chunk(2)
        gate_ref[...] = chunk(3)
    elif mode == "pool":
        u_ref, gate_ref = out_refs
        u_ref[...] = chunk(0)
        gate_ref[...] = chunk(1)
    else:
        glu_ref, gate_ref = out_refs
        glu_ref[...] = chunk(0) * _sigmoid(chunk(1))
        gate_ref[...] = chunk(2)


def _q_scale(d):
    return (d // N_HEADS) ** -0.5 * LOG2E


def _inproj(x, norm_g, w_bf16, mode, tm_want):
    rows, d = x.shape
    n_out = w_bf16.shape[1]
    tm = _row_tile(rows, tm_want)
    row_spec = pl.BlockSpec((tm, d), lambda i: (i, 0))
    dtypes = (BF16, F32, F32, F32) if mode == "attn" else (F32, F32)
    return pl.pallas_call(
        functools.partial(_inproj_kernel, mode=mode, d=d, q_scale=_q_scale(d)),
        grid=(rows // tm,),
        in_specs=[row_spec,
                  pl.BlockSpec((1, d), lambda i: (0, 0)),
                  pl.BlockSpec((d, n_out), lambda i: (0, 0))],
        out_specs=[row_spec] * len(dtypes),
        out_shape=[jax.ShapeDtypeStruct((rows, d), t) for t in dtypes],
        compiler_params=_params("arbitrary"),
        name=f"inproj_{mode}",
    )(x, norm_g.reshape(1, d), w_bf16)


def _inproj_attn_prompt_kernel(x_ref, g_ref, wt_ref, wg_ref,
                               qt_ref, kt_ref, kb_ref, vt_ref, vtb_ref, gate_ref, *, d, q_scale):
    n = _rmsnorm(x_ref[0], g_ref[...]).astype(BF16)

    def chunk_t(c):
        return lax.dot_general(wt_ref[c * d:(c + 1) * d, :], n, _NT, preferred_element_type=F32)

    qt_ref[0, 0] = (chunk_t(0) * q_scale).astype(BF16)
    kt = chunk_t(1)
    kt_ref[0] = kt
    kb_ref[0] = jnp.transpose(kt).astype(BF16)
    vt = chunk_t(2)
    vt_ref[0] = vt
    vtb_ref[0, 0] = vt.astype(BF16)
    gate_ref[0] = jnp.dot(n, wg_ref[...], preferred_element_type=F32)


def _inproj_attn_prompt(x, norm_g, w_in):
    b, s, d = x.shape
    blk = min(ATTN_BLOCK, s)
    assert s % blk == 0
    nt = s // blk
    wt = jnp.transpose(w_in[:, :3 * d]).astype(BF16)
    wg = w_in[:, 3 * d:].astype(BF16)
    row_spec = pl.BlockSpec((1, blk, d), lambda bi, t: (bi, t, 0))
    col_spec = pl.BlockSpec((1, d, blk), lambda bi, t: (bi, 0, t))
    tile_spec = pl.BlockSpec((1, 1, d, blk), lambda bi, t: (bi, t, 0, 0))
    const = lambda shape: pl.BlockSpec(shape, lambda bi, t: (0,) * len(shape))
    return pl.pallas_call(
        functools.partial(_inproj_attn_prompt_kernel, d=d, q_scale=_q_scale(d)),
        grid=(b, nt),
        in_specs=[row_spec, const((1, d)), const((3 * d, d)), const((d, d))],
        out_specs=[tile_spec, col_spec, row_spec, col_spec, tile_spec, row_spec],
        out_shape=[jax.ShapeDtypeStruct((b, nt, d, blk), BF16),
                   jax.ShapeDtypeStruct((b, d, s), F32),
                   jax.ShapeDtypeStruct((b, s, d), BF16),
                   jax.ShapeDtypeStruct((b, d, s), F32),
                   jax.ShapeDtypeStruct((b, nt, d, blk), BF16),
                   jax.ShapeDtypeStruct((b, s, d), F32)],
        compiler_params=_params("arbitrary", "arbitrary"),
        name="inproj_attn_prompt",
    )(x, norm_g.reshape(1, d), wt, wg)


def _gate_out(mix, gate, h, w_out_ref):
    act = (mix * _silu(gate)).astype(BF16)
    return h + jnp.dot(act, w_out_ref[...], preferred_element_type=F32)


def _outproj_kernel(o_ref, gate_ref, h_ref, w_ref, fg_ref, out_ref, *, final_norm):
    h_new = _gate_out(o_ref[...], gate_ref[...], h_ref[...], w_ref)
    if final_norm:
        h_new = _rmsnorm(h_new, fg_ref[...])
    out_ref[...] = h_new


def _outproj(o, gate, h, w_out_bf16, final_g, final_norm, tm_want):
    rows, d = h.shape
    tm = _row_tile(rows, tm_want)
    row_spec = pl.BlockSpec((tm, d), lambda i: (i, 0))
    return pl.pallas_call(
        functools.partial(_outproj_kernel, final_norm=final_norm),
        grid=(rows // tm,),
        in_specs=[row_spec, row_spec, row_spec,
                  pl.BlockSpec((d, d), lambda i: (0, 0)),
                  pl.BlockSpec((1, d), lambda i: (0, 0))],
        out_specs=row_spec,
        out_shape=jax.ShapeDtypeStruct((rows, d), F32),
        compiler_params=_params("arbitrary"),
        name="outproj_final" if final_norm else "outproj",
    )(o, gate, h, w_out_bf16, final_g.reshape(1, d))


def _attn_prompt_kernel(bias_ref, qt_ref, k_ref, vt_ref, o_ref, acc_ref, carry_ref, y_buf, *, blk, n_heads):
    hp = pl.program_id(1)
    qi = pl.program_id(2)
    grp = ATTN_GROUP
    half = LANES // HEADS_PER_BLOCK
    chan = lax.broadcasted_iota(jnp.int32, (LANES, 1), 0)
    first_head = chan < half

    qt = qt_ref[0, 0]
    zero = jnp.zeros_like(qt)
    qq = jnp.concatenate([jnp.where(first_head, qt, zero), jnp.where(first_head, zero, qt)], axis=1)
    col2 = lax.broadcasted_iota(jnp.int32, (1, 2 * blk), 1)

    def bias_piece(p):
        return jnp.where(col2 < blk, bias_ref[p * n_heads + 2 * hp], bias_ref[p * n_heads + 2 * hp + 1])

    bias_rows = jnp.where(chan == 0, bias_piece(0),
                          jnp.where(chan == 1, bias_piece(1),
                                    jnp.where(chan == 2, bias_piece(2), 0.0))).astype(BF16)
    qq_ext = jnp.concatenate([qq, bias_rows], axis=0)
    lane = lax.broadcasted_iota(jnp.int32, (blk, LANES), 1)
    ones_cols = (lane < 3).astype(BF16)

    ss = lax.broadcasted_iota(jnp.int32, (blk, blk), 0)
    jj = lax.broadcasted_iota(jnp.int32, (blk, blk), 1)
    suffix_ones = (jj >= ss).astype(BF16)

    acc_ref[...] = jnp.zeros_like(acc_ref)
    carry_ref[...] = jnp.zeros_like(carry_ref)

    def tiles(kbs, diagonal):
        n = len(kbs)
        for i, kb in enumerate(kbs):
            ks = k_ref[0, pl.ds(pl.multiple_of(kb * blk, blk), blk), :]
            ks_ext = jnp.concatenate([ks, ones_cols], axis=1)
            y_buf[i] = jnp.dot(ks_ext, qq_ext, preferred_element_type=F32)
        visible = None
        if any(diagonal):
            r = lax.broadcasted_iota(jnp.int32, (blk, 2 * blk), 0)
            c = lax.broadcasted_iota(jnp.int32, (blk, 2 * blk), 1)
            visible = r < jnp.where(c < blk, c, c - blk)
        parts = []

        def second_half(i, c_local):
            a = jnp.exp2(y_buf[i] - c_local)
            if diagonal[i]:
                a = jnp.where(visible, a, 0.0)
            ab = a.astype(BF16)
            vt = vt_ref[0, kbs[i]]
            p0 = jnp.dot(vt[:half], ab[:, :blk], preferred_element_type=F32)
            p1 = jnp.dot(vt[half:], ab[:, blk:], preferred_element_type=F32)
            parts.append((p0, p1, c_local[0:1, :]))

        pending = None
        for i in range(n):
            sp = _softplus2(y_buf[i])
            if diagonal[i]:
                sp = jnp.where(visible, sp, 0.0)
            c_local = jnp.dot(suffix_ones, sp.astype(BF16), preferred_element_type=F32)
            if pending is not None:
                second_half(*pending)
            pending = (i, c_local)
        second_half(*pending)
        return parts

    def accumulate(parts):
        for p0, p1, total in parts:
            carry = carry_ref[...]
            decay = jnp.exp2(-carry)
            acc_ref[0:half, :] += p0 * decay[:, :blk]
            acc_ref[half:, :] += p1 * decay[:, blk:]
            carry_ref[...] = carry + total

    for lead in range(grp):
        @pl.when(qi % grp == lead)
        def _(lead=lead):
            accumulate(tiles([qi - i for i in range(lead + 1)], [True] + [False] * lead))

    def group(g, _):
        top = qi - qi % grp - 1 - grp * g
        accumulate(tiles([top - i for i in range(grp)], [False] * grp))
        return 0

    lax.fori_loop(0, qi // grp, group, 0)

    o_ref[0] = jnp.transpose(acc_ref[...])


def _attn_prompt(qt, kb, vtb, bias2):
    b, nt, d, blk = qt.shape
    s = nt * blk
    n_pairs = d // LANES
    hi = bias2.astype(BF16).astype(F32)
    mid = (bias2 - hi).astype(BF16).astype(F32)
    lo = (bias2 - hi - mid).astype(BF16).astype(F32)
    pieces = jnp.concatenate([hi, mid, lo])
    return pl.pallas_call(
        functools.partial(_attn_prompt_kernel, blk=blk, n_heads=bias2.shape[0]),
        grid_spec=pltpu.PrefetchScalarGridSpec(
            num_scalar_prefetch=1,
            grid=(b, n_pairs, nt),
            in_specs=[pl.BlockSpec((1, 1, LANES, blk), lambda bi, hp, qi, bias: (bi, qi, hp, 0)),
                      pl.BlockSpec((1, s, LANES), lambda bi, hp, qi, bias: (bi, 0, hp)),
                      pl.BlockSpec((1, nt, LANES, blk), lambda bi, hp, qi, bias: (bi, 0, hp, 0))],
            out_specs=pl.BlockSpec((1, blk, LANES), lambda bi, hp, qi, bias: (bi, qi, hp)),
            scratch_shapes=[pltpu.VMEM((LANES, blk), F32), pltpu.VMEM((1, 2 * blk), F32),
                            pltpu.VMEM((ATTN_GROUP, blk, 2 * blk), F32)]),
        out_shape=jax.ShapeDtypeStruct((b, s, d), F32),
        compiler_params=_params("arbitrary", "arbitrary", "arbitrary"),
        name="attn_prompt",
    )(pieces, qt, kb, vtb)


def _attn_decode_kernel(pt_ref, q_ref, bias_ref, *refs, n_pages_step):
    k_refs = refs[:n_pages_step]
    v_refs = refs[n_pages_step:2 * n_pages_step]
    o_ref, qb_ref, acc_ref, carry_ref, y_ref, a_ref = refs[2 * n_pages_step:]
    step = pl.program_id(1)

    @pl.when(step == 0)
    def _():
        qb_ref[...] = jnp.broadcast_to(q_ref[0], qb_ref.shape)
        acc_ref[...] = jnp.zeros_like(acc_ref)
        carry_ref[...] = jnp.zeros_like(carry_ref)

    jj = lax.broadcasted_iota(jnp.int32, (PAGE_SIZE, PAGE_SIZE), 0)
    ss = lax.broadcasted_iota(jnp.int32, (PAGE_SIZE, PAGE_SIZE), 1)
    suffix_ones = (jj >= ss).astype(BF16)
    bias = bias_ref[...]
    carry = carry_ref[...]

    for p in range(n_pages_step):
        for h in range(N_HEADS):
            y_ref[h:h + 1, :] = jnp.sum(qb_ref[h] * k_refs[p][h], axis=0, keepdims=True)
        y = y_ref[...] + bias
        sp = _softplus2(y)
        c_local = jnp.dot(sp.astype(BF16), suffix_ones, preferred_element_type=F32)
        a_ref[p] = jnp.exp2(y - c_local - carry)
        carry = carry + c_local[:, 0:1]
    carry_ref[...] = carry

    for h in range(N_HEADS):
        acc = acc_ref[h]
        for p in range(n_pages_step):
            acc = acc + a_ref[p, h:h + 1, :] * v_refs[p][h]
        acc_ref[h] = acc

    @pl.when(step == pl.num_programs(1) - 1)
    def _():
        o_ref[0] = jnp.sum(acc_ref[...], axis=2, keepdims=True)


def _attn_decode(qs, cache_kt, cache_vt, layer, page_table, bias2):
    db, d = qs.shape
    dh = d // N_HEADS
    n_pages = page_table.shape[1]
    pps = min(DECODE_PAGES_PER_STEP, n_pages)
    assert n_pages % pps == 0

    def page_spec(p):
        def index(bi, st, pt):
            return (layer, pt[bi * n_pages + (n_pages - 1 - (st * pps + p))], 0, 0, 0)
        return pl.BlockSpec((None, None, N_HEADS, dh, PAGE_SIZE), index)

    specs = [page_spec(p) for p in range(pps)]
    out = pl.pallas_call(
        functools.partial(_attn_decode_kernel, n_pages_step=pps),
        grid_spec=pltpu.PrefetchScalarGridSpec(
            num_scalar_prefetch=1,
            grid=(db, n_pages // pps),
            in_specs=[pl.BlockSpec((1, N_HEADS, dh, 1), lambda bi, st, pt: (bi, 0, 0, 0)),
                      pl.BlockSpec((N_HEADS, 1), lambda bi, st, pt: (0, 0))] + specs + specs,
            out_specs=pl.BlockSpec((1, N_HEADS, dh, 1), lambda bi, st, pt: (bi, 0, 0, 0)),
            scratch_shapes=[pltpu.VMEM((N_HEADS, dh, PAGE_SIZE), F32),
                            pltpu.VMEM((N_HEADS, dh, PAGE_SIZE), F32),
                            pltpu.VMEM((N_HEADS, 1), F32),
                            pltpu.VMEM((N_HEADS, PAGE_SIZE), F32),
                            pltpu.VMEM((pps, N_HEADS, PAGE_SIZE), F32)]),
        out_shape=jax.ShapeDtypeStruct((db, N_HEADS, dh, 1), F32),
        compiler_params=_params("arbitrary", "arbitrary"),
        name="attn_decode",
    )(page_table.reshape(-1), qs.reshape(db, N_HEADS, dh, 1), bias2.reshape(N_HEADS, 1),
      *([cache_kt] * pps), *([cache_vt] * pps))
    return out.reshape(db, d)


def _pool_groups(window_sum, u_of, inv_cnt_of, wg_ref, scale):
    outs = []
    for gi, w in enumerate(POOL_WINDOWS):
        mixed = (window_sum(gi, w) * inv_cnt_of(w) - u_of(gi)).astype(BF16)
        outs.append(jnp.dot(mixed, wg_ref[gi], preferred_element_type=F32))
    return jnp.concatenate(outs, axis=-1) * scale


def _pool_prompt_kernel(u_ref, prev_ref, gate_ref, h_ref, wg_ref, scale_ref, w_ref, out_ref, ext_ref,
                        *, tm, grp):
    t = pl.program_id(1)
    prev = prev_ref[0]
    ext_ref[0:POOL_PREV_ROWS, :] = jnp.where(t > 0, prev, jnp.zeros_like(prev))
    ext_ref[POOL_PREV_ROWS:, :] = u_ref[0]
    pos = t * tm + lax.broadcasted_iota(jnp.int32, (tm, 1), 0)

    def window_sum(gi, w):
        sl = slice(gi * grp, (gi + 1) * grp)
        s = ext_ref[POOL_PREV_ROWS:POOL_PREV_ROWS + tm, sl]
        for i in range(1, w):
            s = s + ext_ref[POOL_PREV_ROWS - i:POOL_PREV_ROWS - i + tm, sl]
        return s

    def u_of(gi):
        return ext_ref[POOL_PREV_ROWS:POOL_PREV_ROWS + tm, gi * grp:(gi + 1) * grp]

    def inv_cnt_of(w):
        return 1.0 / jnp.minimum(w, pos + 1).astype(F32)

    mix = _pool_groups(window_sum, u_of, inv_cnt_of, wg_ref, scale_ref[...])
    out_ref[0] = _gate_out(mix, gate_ref[0], h_ref[0], w_ref)


def _pool_prompt(u, gate, h, wg_bf16, scale, w_out_bf16, tm_want):
    b, s, d = u.shape
    tm = _row_tile(s, tm_want)
    assert tm % POOL_PREV_ROWS == 0
    grp = d // len(POOL_WINDOWS)
    ratio = tm // POOL_PREV_ROWS
    row_spec = pl.BlockSpec((1, tm, d), lambda bi, t: (bi, t, 0))
    return pl.pallas_call(
        functools.partial(_pool_prompt_kernel, tm=tm, grp=grp),
        grid=(b, s // tm),
        in_specs=[row_spec,
                  pl.BlockSpec((1, POOL_PREV_ROWS, d), lambda bi, t: (bi, jnp.maximum(t * ratio - 1, 0), 0)),
                  row_spec, row_spec,
                  pl.BlockSpec((len(POOL_WINDOWS), grp, grp), lambda bi, t: (0, 0, 0)),
                  pl.BlockSpec((1, d), lambda bi, t: (0, 0)),
                  pl.BlockSpec((d, d), lambda bi, t: (0, 0))],
        out_specs=row_spec,
        out_shape=jax.ShapeDtypeStruct((b, s, d), F32),
        scratch_shapes=[pltpu.VMEM((tm + POOL_PREV_ROWS, d), F32)],
        compiler_params=_params("arbitrary", "arbitrary"),
        name="pool_prompt",
    )(u, u, gate, h, wg_bf16, scale.reshape(1, d), w_out_bf16)


def _pool_sample_kernel(hist_ref, u_ref, gate_ref, h_ref, wg_ref, scale_ref, w_ref, out_ref, *, grp, past):
    n_hist = hist_ref.shape[0]

    def window_sum(gi, w):
        sl = slice(gi * grp, (gi + 1) * grp)
        s = u_ref[:, sl]
        for i in range(1, w):
            s = s + hist_ref[n_hist - i, :, sl]
        return s

    def u_of(gi):
        return u_ref[:, gi * grp:(gi + 1) * grp]

    def inv_cnt_of(w):
        return 1.0 / float(min(w, past + 1))

    mix = _pool_groups(window_sum, u_of, inv_cnt_of, wg_ref, scale_ref[...])
    out_ref[...] = _gate_out(mix, gate_ref[...], h_ref[...], w_ref)


def _pool_sample(hist_t, u, gate, h, wg_bf16, scale, w_out_bf16, past):
    db, d = u.shape
    grp = d // len(POOL_WINDOWS)
    return pl.pallas_call(
        functools.partial(_pool_sample_kernel, grp=grp, past=past),
        out_shape=jax.ShapeDtypeStruct((db, d), F32),
        compiler_params=pltpu.CompilerParams(vmem_limit_bytes=VMEM_LIMIT_BYTES),
        name="pool_sample",
    )(hist_t, u, gate, h, wg_bf16, scale.reshape(1, d), w_out_bf16)


def _ln_silu(y, ln_g, ln_b):
    mu = jnp.mean(y, axis=-1, keepdims=True)
    yc = y - mu
    yn = yc * lax.rsqrt(jnp.mean(yc * yc, axis=-1, keepdims=True) + LN_EPS) * ln_g + ln_b
    return _silu(yn)


def _conv_prompt_kernel(glu_ref, prev_ref, gate_ref, h_ref, dw_ref, db_ref, lg_ref, lb_ref, w_ref,
                        out_ref, ext_ref, shifted_ref, mix_ref, *, tm):
    t = pl.program_id(1)
    prev = prev_ref[0]
    ext_ref[0:CONV_PREV_ROWS, :] = jnp.where(t > 0, prev, jnp.zeros_like(prev))
    ext_ref[CONV_PREV_ROWS:, :] = glu_ref[0]
    lead = CONV_PREV_ROWS - (CONV_WIDTH - 1)
    n_shifted = shifted_ref.shape[1]
    for shift in range(1, SUBLANES):
        shifted_ref[shift - 1] = ext_ref[shift:shift + n_shifted, :]
    for c in range(tm // CONV_ROW_CHUNK):
        r0 = c * CONV_ROW_CHUNK
        y = None
        for i in range(CONV_WIDTH):
            shift = (lead + i) % SUBLANES
            q = r0 + lead + i - shift
            if shift == 0:
                rows = ext_ref[q:q + CONV_ROW_CHUNK, :]
            else:
                rows = shifted_ref[shift - 1, q:q + CONV_ROW_CHUNK, :]
            term = dw_ref[i:i + 1, :] * rows
            y = term if y is None else y + term
        mix_ref[r0:r0 + CONV_ROW_CHUNK, :] = _ln_silu(y + db_ref[...], lg_ref[...], lb_ref[...])
    out_ref[0] = _gate_out(mix_ref[...], gate_ref[0], h_ref[0], w_ref)


def _conv_prompt(glu, gate, h, dw_w, dw_b, ln_g, ln_b, w_out_bf16, tm_want):
    b, s, d = glu.shape
    tm = _row_tile(s, tm_want)
    assert tm % CONV_PREV_ROWS == 0 and tm % CONV_ROW_CHUNK == 0
    ratio = tm // CONV_PREV_ROWS
    row_spec = pl.BlockSpec((1, tm, d), lambda bi, t: (bi, t, 0))
    vec_spec = pl.BlockSpec((1, d), lambda bi, t: (0, 0))
    return pl.pallas_call(
        functools.partial(_conv_prompt_kernel, tm=tm),
        grid=(b, s // tm),
        in_specs=[row_spec,
                  pl.BlockSpec((1, CONV_PREV_ROWS, d), lambda bi, t: (bi, jnp.maximum(t * ratio - 1, 0), 0)),
                  row_spec, row_spec,
                  pl.BlockSpec((CONV_WIDTH, d), lambda bi, t: (0, 0)),
                  vec_spec, vec_spec, vec_spec,
                  pl.BlockSpec((d, d), lambda bi, t: (0, 0))],
        out_specs=row_spec,
        out_shape=jax.ShapeDtypeStruct((b, s, d), F32),
        scratch_shapes=[pltpu.VMEM((tm + CONV_PREV_ROWS, d), F32),
                        pltpu.VMEM((SUBLANES - 1, tm + CONV_PREV_ROWS - SUBLANES, d), F32),
                        pltpu.VMEM((tm, d), F32)],
        compiler_params=_params("arbitrary", "arbitrary"),
        name="conv_prompt",
    )(glu, glu, gate, h, dw_w, dw_b.reshape(1, d), ln_g.reshape(1, d), ln_b.reshape(1, d), w_out_bf16)


def _conv_sample_kernel(hist_ref, glu_ref, gate_ref, h_ref, dw_ref, db_ref, lg_ref, lb_ref, w_ref, out_ref):
    n_hist = hist_ref.shape[0]
    y = dw_ref[n_hist:n_hist + 1, :] * glu_ref[...]
    for i in range(n_hist):
        y = y + dw_ref[i:i + 1, :] * hist_ref[i]
    mix = _ln_silu(y + db_ref[...], lg_ref[...], lb_ref[...])
    out_ref[...] = _gate_out(mix, gate_ref[...], h_ref[...], w_ref)


def _conv_sample(hist_t, glu, gate, h, dw_w, dw_b, ln_g, ln_b, w_out_bf16):
    db, d = glu.shape
    return pl.pallas_call(
        _conv_sample_kernel,
        out_shape=jax.ShapeDtypeStruct((db, d), F32),
        compiler_params=pltpu.CompilerParams(vmem_limit_bytes=VMEM_LIMIT_BYTES),
        name="conv_sample",
    )(hist_t, glu, gate, h, dw_w, dw_b.reshape(1, d), ln_g.reshape(1, d), ln_b.reshape(1, d), w_out_bf16)


def kernel(x_prompt, x_sample, cache_k, cache_v, state_pool, state_conv, page_table, norm_g, final_norm_g,
           attn_w_in, attn_bias, attn_w_out, pool_w_in, pool_w_group, pool_scale, pool_w_out, conv_w_in,
           conv_dw_w, conv_dw_b, conv_ln_g, conv_ln_b, conv_w_out):
    b, s, d = x_prompt.shape
    db, ds, _ = x_sample.shape
    assert ds == 1 and d % LANES == 0 and d // N_HEADS * HEADS_PER_BLOCK == LANES
    depth = norm_g.shape[0]
    past = page_table.shape[1] * PAGE_SIZE
    dh = d // N_HEADS
    pool_hist = state_pool.shape[2]
    conv_hist = state_conv.shape[2]
    assert pool_hist == max(POOL_WINDOWS) - 1 and conv_hist == CONV_WIDTH - 1

    cache_kt = jnp.transpose(cache_k, (0, 1, 3, 4, 2))
    cache_vt = jnp.transpose(cache_v, (0, 1, 3, 4, 2))

    def heads_last(xt):
        return jnp.transpose(xt.reshape(b, N_HEADS, dh, s), (0, 3, 1, 2))

    hp = x_prompt
    hs = x_sample.reshape(db, d)
    kp_l, vp_l, ks_l, vs_l = [], [], [], []
    poolp_l, pools_l, convp_l, convs_l = [], [], [], []
    for i in range(depth):
        kind = i % N_MIXERS
        j = i // N_MIXERS
        last = i == depth - 1
        if kind == 0:
            w_out = attn_w_out[j].astype(BF16)
            bias2 = attn_bias[j] * LOG2E
            qt, kt, kb, vt, vtb, gp = _inproj_attn_prompt(hp, norm_g[i], attn_w_in[j])
            qsb, ks_, vs, gs = _inproj(hs, norm_g[i], attn_w_in[j].astype(BF16), "attn", 256)
            op = _attn_prompt(qt, kb, vtb, bias2)
            os_ = _attn_decode(qsb.astype(F32), cache_kt, cache_vt, j, page_table, bias2)
            hp = _outproj(op.reshape(b * s, d), gp.reshape(b * s, d), hp.reshape(b * s, d), w_out,
                          final_norm_g, last, 512).reshape(b, s, d)
            hs = _outproj(os_, gs, hs, w_out, final_norm_g, last, 512)
            kp_l.append(heads_last(kt))
            vp_l.append(heads_last(vt))
            ks_l.append(ks_.reshape(db, ds, N_HEADS, dh))
            vs_l.append(vs.reshape(db, ds, N_HEADS, dh))
        elif kind == 1:
            w_in = pool_w_in[j].astype(BF16)
            w_out = pool_w_out[j].astype(BF16)
            wg = pool_w_group[j].astype(BF16)
            up, gp = _inproj(hp.reshape(b * s, d), norm_g[i], w_in, "pool", 512)
            us, gs = _inproj(hs, norm_g[i], w_in, "pool", 512)
            up3 = up.reshape(b, s, d)
            hp = _pool_prompt(up3, gp.reshape(b, s, d), hp, wg, pool_scale[j], w_out, 256)
            hist_t = jnp.transpose(state_pool[j], (1, 0, 2))
            hs = _pool_sample(hist_t, us, gs, hs, wg, pool_scale[j], w_out, past)
            if s >= pool_hist:
                poolp_l.append(up3[:, s - pool_hist:])
            else:
                poolp_l.append(jnp.concatenate([jnp.zeros((b, pool_hist - s, d), F32), up3], axis=1))
            pools_l.append(jnp.concatenate([state_pool[j][:, ds:], us.reshape(db, ds, d)], axis=1))
        else:
            w_in = conv_w_in[j].astype(BF16)
            w_out = conv_w_out[j].astype(BF16)
            glu_p, gp = _inproj(hp.reshape(b * s, d), norm_g[i], w_in, "conv", 512)
            glu_s, gs = _inproj(hs, norm_g[i], w_in, "conv", 512)
            glu3 = glu_p.reshape(b, s, d)
            hp = _conv_prompt(glu3, gp.reshape(b, s, d), hp, conv_dw_w[j], conv_dw_b[j],
                              conv_ln_g[j], conv_ln_b[j], w_out, 256)
            hist_t = jnp.transpose(state_conv[j], (1, 0, 2))
            hs = _conv_sample(hist_t, glu_s, gs, hs, conv_dw_w[j], conv_dw_b[j], conv_ln_g[j], conv_ln_b[j],
                              w_out)
            if s >= conv_hist:
                convp_l.append(glu3[:, s - conv_hist:])
            else:
                convp_l.append(jnp.concatenate([jnp.zeros((b, conv_hist - s, d), F32), glu3], axis=1))
            convs_l.append(jnp.concatenate([state_conv[j][:, ds:], glu_s.reshape(db, ds, d)], axis=1))
    assert depth % N_MIXERS == 1, "the final RMSNorm is fused into the attention output projection"
    return (hp, hs.reshape(db, ds, d), jnp.stack(kp_l), jnp.stack(vp_l), jnp.stack(ks_l),
            jnp.stack(vs_l), jnp.stack(poolp_l), jnp.stack(pools_l), jnp.stack(convp_l), jnp.stack(convs_l))
```

```python
import functools
import math

import jax
import jax.numpy as jnp
from jax import lax
from jax.experimental import pallas as pl
from jax.experimental.pallas import tpu as pltpu

F32 = jnp.float32
BF16 = jnp.bfloat16

N_MIXERS = 3
N_HEADS = 16
PAGE_SIZE = 128
POOL_WINDOWS = (2, 4, 8, 16)
CONV_WIDTH = 31
RMS_EPS = 1e-6
LN_EPS = 1e-5
LOG2E = math.log2(math.e)

LANES = 128
SUBLANES = 8
HEADS_PER_BLOCK = 2
VMEM_LIMIT_BYTES = 56 * 1024 * 1024
ATTN_BLOCK = 256
ATTN_GROUP = 6
ATTN_QBLOCKS_PER_STEP = 4
DECODE_PAGES_PER_STEP = 8
POOL_PREV_ROWS = 16
CONV_PREV_ROWS = 32
CONV_ROW_CHUNK = 32

_NT = (((1,), (1,)), ((), ()))


def _row_tile(rows, want):
    tm = min(rows, want)
    assert rows % tm == 0 and tm % 8 == 0, (rows, tm)
    return tm


def _params(*sem):
    return pltpu.CompilerParams(dimension_semantics=sem, vmem_limit_bytes=VMEM_LIMIT_BYTES)


def _sigmoid(x):
    return 1.0 / (1.0 + jnp.exp(-x))


def _silu(x):
    return x * _sigmoid(x)


def _softplus2(y):
    neg_abs = lax.bitcast_convert_type(lax.bitcast_convert_type(y, jnp.int32) | jnp.int32(-2 ** 31), F32)
    return jnp.maximum(y, 0.0) + jnp.log(1.0 + jnp.exp2(neg_abs)) * LOG2E


def _rmsnorm(x, g):
    return x * lax.rsqrt(jnp.mean(x * x, axis=-1, keepdims=True) + RMS_EPS) * g


def _inproj_kernel(x_ref, g_ref, w_ref, *out_refs, mode, d, q_scale):
    n = _rmsnorm(x_ref[...], g_ref[...]).astype(BF16)

    def chunk(c):
        return jnp.dot(n, w_ref[:, c * d:(c + 1) * d], preferred_element_type=F32)

    if mode == "attn":
        qb_ref, k_ref, v_ref, gate_ref = out_refs
        qb_ref[...] = (chunk(0) * q_scale).astype(BF16)
        k_ref[...] = chunk(1)
        v_ref[...] = chunk(2)
        gate_ref[...] = chunk(3)
    elif mode == "pool":
        u_ref, gate_ref = out_refs
        u_ref[...] = chunk(0)
        gate_ref[...] = chunk(1)
    else:
        glu_ref, gate_ref = out_refs
        glu_ref[...] = chunk(0) * _sigmoid(chunk(1))
        gate_ref[...] = chunk(2)


def _q_scale(d):
    return (d // N_HEADS) ** -0.5 * LOG2E


def _inproj(x, norm_g, w_bf16, mode, tm_want):
    rows, d = x.shape
    n_out = w_bf16.shape[1]
    tm = _row_tile(rows, tm_want)
    row_spec = pl.BlockSpec((tm, d), lambda i: (i, 0))
    dtypes = (BF16, F32, F32, F32) if mode == "attn" else (F32, F32)
    return pl.pallas_call(
        functools.partial(_inproj_kernel, mode=mode, d=d, q_scale=_q_scale(d)),
        grid=(rows // tm,),
        in_specs=[row_spec,
                  pl.BlockSpec((1, d), lambda i: (0, 0)),
                  pl.BlockSpec((d, n_out), lambda i: (0, 0))],
        out_specs=[row_spec] * len(dtypes),
        out_shape=[jax.ShapeDtypeStruct((rows, d), t) for t in dtypes],
        compiler_params=_params("arbitrary"),
        name=f"inproj_{mode}",
    )(x, norm_g.reshape(1, d), w_bf16)


def _inproj_attn_prompt_kernel(x_ref, g_ref, wt_ref, wg_ref,
                               qt_ref, kt_ref, kb_ref, vt_ref, vtb_ref, gate_ref, *, d, q_scale):
    n = _rmsnorm(x_ref[0], g_ref[...]).astype(BF16)

    def chunk_t(c):
        return lax.dot_general(wt_ref[c * d:(c + 1) * d, :], n, _NT, preferred_element_type=F32)

    qt_ref[0, 0] = (chunk_t(0) * q_scale).astype(BF16)
    kt = chunk_t(1)
    kt_ref[0] = kt
    kb_ref[0] = jnp.transpose(kt).astype(BF16)
    vt = chunk_t(2)
    vt_ref[0] = vt
    vtb_ref[0, 0] = vt.astype(BF16)
    gate_ref[0] = jnp.dot(n, wg_ref[...], preferred_element_type=F32)


def _inproj_attn_prompt(x, norm_g, w_in):
    b, s, d = x.shape
    blk = min(ATTN_BLOCK, s)
    assert s % blk == 0
    nt = s // blk
    wt = jnp.transpose(w_in[:, :3 * d]).astype(BF16)
    wg = w_in[:, 3 * d:].astype(BF16)
    row_spec = pl.BlockSpec((1, blk, d), lambda bi, t: (bi, t, 0))
    col_spec = pl.BlockSpec((1, d, blk), lambda bi, t: (bi, 0, t))
    tile_spec = pl.BlockSpec((1, 1, d, blk), lambda bi, t: (bi, t, 0, 0))
    const = lambda shape: pl.BlockSpec(shape, lambda bi, t: (0,) * len(shape))
    return pl.pallas_call(
        functools.partial(_inproj_attn_prompt_kernel, d=d, q_scale=_q_scale(d)),
        grid=(b, nt),
        in_specs=[row_spec, const((1, d)), const((3 * d, d)), const((d, d))],
        out_specs=[tile_spec, col_spec, row_spec, col_spec, tile_spec, row_spec],
        out_shape=[jax.ShapeDtypeStruct((b, nt, d, blk), BF16),
                   jax.ShapeDtypeStruct((b, d, s), F32),
                   jax.ShapeDtypeStruct((b, s, d), BF16),
                   jax.ShapeDtypeStruct((b, d, s), F32),
                   jax.ShapeDtypeStruct((b, nt, d, blk), BF16),
                   jax.ShapeDtypeStruct((b, s, d), F32)],
        compiler_params=_params("arbitrary", "arbitrary"),
        name="inproj_attn_prompt",
    )(x, norm_g.reshape(1, d), wt, wg)


def _gate_out(mix, gate, h, w_out_ref):
    act = (mix * _silu(gate)).astype(BF16)
    return h + jnp.dot(act, w_out_ref[...], preferred_element_type=F32)


def _outproj_kernel(o_ref, gate_ref, h_ref, w_ref, fg_ref, out_ref, *, final_norm):
    h_new = _gate_out(o_ref[...], gate_ref[...], h_ref[...], w_ref)
    if final_norm:
        h_new = _rmsnorm(h_new, fg_ref[...])
    out_ref[...] = h_new


def _outproj(o, gate, h, w_out_bf16, final_g, final_norm, tm_want):
    rows, d = h.shape
    tm = _row_tile(rows, tm_want)
    row_spec = pl.BlockSpec((tm, d), lambda i: (i, 0))
    return pl.pallas_call(
        functools.partial(_outproj_kernel, final_norm=final_norm),
        grid=(rows // tm,),
        in_specs=[row_spec, row_spec, row_spec,
                  pl.BlockSpec((d, d), lambda i: (0, 0)),
                  pl.BlockSpec((1, d), lambda i: (0, 0))],
        out_specs=row_spec,
        out_shape=jax.ShapeDtypeStruct((rows, d), F32),
        compiler_params=_params("arbitrary"),
        name="outproj_final" if final_norm else "outproj",
    )(o, gate, h, w_out_bf16, final_g.reshape(1, d))


def _attn_prompt_kernel(bias_ref, qt_ref, k_ref, vt_ref, o_ref, acc_ref, carry_ref, y_buf,
                        *, blk, n_heads, n_qblocks):
    hp = pl.program_id(1)
    grp = ATTN_GROUP
    half = LANES // HEADS_PER_BLOCK
    chan = lax.broadcasted_iota(jnp.int32, (LANES, 1), 0)
    first_head = chan < half

    col2 = lax.broadcasted_iota(jnp.int32, (1, 2 * blk), 1)

    def bias_piece(p):
        return jnp.where(col2 < blk, bias_ref[p * n_heads + 2 * hp], bias_ref[p * n_heads + 2 * hp + 1])

    bias_rows = jnp.where(chan == 0, bias_piece(0),
                          jnp.where(chan == 1, bias_piece(1),
                                    jnp.where(chan == 2, bias_piece(2), 0.0))).astype(BF16)
    lane = lax.broadcasted_iota(jnp.int32, (blk, LANES), 1)
    ones_cols = (lane < 3).astype(BF16)

    ss = lax.broadcasted_iota(jnp.int32, (blk, blk), 0)
    jj = lax.broadcasted_iota(jnp.int32, (blk, blk), 1)
    suffix_ones = (jj >= ss).astype(BF16)

    def per_query_block(j, _):
        qi = pl.program_id(2) * n_qblocks + j
        qt = qt_ref[0, j]
        zero = jnp.zeros_like(qt)
        qq = jnp.concatenate([jnp.where(first_head, qt, zero), jnp.where(first_head, zero, qt)], axis=1)
        qq_ext = jnp.concatenate([qq, bias_rows], axis=0)
        acc_ref[...] = jnp.zeros_like(acc_ref)
        carry_ref[...] = jnp.zeros_like(carry_ref)

        def tiles(kbs, diagonal):
            n = len(kbs)
            for i, kb in enumerate(kbs):
                ks = k_ref[0, pl.ds(pl.multiple_of(kb * blk, blk), blk), :]
                ks_ext = jnp.concatenate([ks, ones_cols], axis=1)
                y_buf[i] = jnp.dot(ks_ext, qq_ext, preferred_element_type=F32)
            if any(diagonal):
                r = lax.broadcasted_iota(jnp.int32, (blk, 2 * blk), 0)
                c = lax.broadcasted_iota(jnp.int32, (blk, 2 * blk), 1)
                visible = r < jnp.where(c < blk, c, c - blk)
            parts = []

            def second_half(i, c_local):
                a = jnp.exp2(y_buf[i] - c_local)
                if diagonal[i]:
                    a = jnp.where(visible, a, 0.0)
                ab = a.astype(BF16)
                vt = vt_ref[0, kbs[i]]
                p0 = jnp.dot(vt[:half], ab[:, :blk], preferred_element_type=F32)
                p1 = jnp.dot(vt[half:], ab[:, blk:], preferred_element_type=F32)
                parts.append((p0, p1, c_local[0:1, :]))

            pending = None
            for i in range(n):
                sp = _softplus2(y_buf[i])
                if diagonal[i]:
                    sp = jnp.where(visible, sp, 0.0)
                c_local = jnp.dot(suffix_ones, sp.astype(BF16), preferred_element_type=F32)
                if pending is not None:
                    second_half(*pending)
                pending = (i, c_local)
            second_half(*pending)
            return parts

        def accumulate(parts):
            for p0, p1, total in parts:
                carry = carry_ref[...]
                decay = jnp.exp2(-carry)
                acc_ref[0:half, :] += p0 * decay[:, :blk]
                acc_ref[half:, :] += p1 * decay[:, blk:]
                carry_ref[...] = carry + total

        for lead in range(grp):
            @pl.when(qi % grp == lead)
            def _(lead=lead):
                accumulate(tiles([qi - i for i in range(lead + 1)], [True] + [False] * lead))

        def group(g, _):
            top = qi - qi % grp - 1 - grp * g
            accumulate(tiles([top - i for i in range(grp)], [False] * grp))
            return 0

        lax.fori_loop(0, qi // grp, group, 0)

        o_ref[0, pl.ds(pl.multiple_of(j * blk, blk), blk), :] = jnp.transpose(acc_ref[...])
        return 0

    lax.fori_loop(0, n_qblocks, per_query_block, 0)


def _attn_prompt(qt, kb, vtb, bias2):
    b, nt, d, blk = qt.shape
    s = nt * blk
    n_pairs = d // LANES
    hi = bias2.astype(BF16).astype(F32)
    mid = (bias2 - hi).astype(BF16).astype(F32)
    lo = (bias2 - hi - mid).astype(BF16).astype(F32)
    pieces = jnp.concatenate([hi, mid, lo])
    nq = math.gcd(ATTN_QBLOCKS_PER_STEP, nt)
    return pl.pallas_call(
        functools.partial(_attn_prompt_kernel, blk=blk, n_heads=bias2.shape[0], n_qblocks=nq),
        grid_spec=pltpu.PrefetchScalarGridSpec(
            num_scalar_prefetch=1,
            grid=(b, n_pairs, nt // nq),
            in_specs=[pl.BlockSpec((1, nq, LANES, blk), lambda bi, hp, qs, bias: (bi, qs, hp, 0)),
                      pl.BlockSpec((1, s, LANES), lambda bi, hp, qs, bias: (bi, 0, hp)),
                      pl.BlockSpec((1, nt, LANES, blk), lambda bi, hp, qs, bias: (bi, 0, hp, 0))],
            out_specs=pl.BlockSpec((1, nq * blk, LANES), lambda bi, hp, qs, bias: (bi, qs, hp)),
            scratch_shapes=[pltpu.VMEM((LANES, blk), F32), pltpu.VMEM((1, 2 * blk), F32),
                            pltpu.VMEM((ATTN_GROUP, blk, 2 * blk), F32)]),
        out_shape=jax.ShapeDtypeStruct((b, s, d), F32),
        compiler_params=_params("arbitrary", "arbitrary", "arbitrary"),
        name="attn_prompt",
    )(pieces, qt, kb, vtb)


def _attn_decode_kernel(pt_ref, q_ref, bias_ref, *refs, n_pages_step):
    k_refs = refs[:n_pages_step]
    v_refs = refs[n_pages_step:2 * n_pages_step]
    o_ref, qb_ref, acc_ref, carry_ref, y_ref, a_ref = refs[2 * n_pages_step:]
    step = pl.program_id(1)

    @pl.when(step == 0)
    def _():
        qb_ref[...] = jnp.broadcast_to(q_ref[0], qb_ref.shape)
        acc_ref[...] = jnp.zeros_like(acc_ref)
        carry_ref[...] = jnp.zeros_like(carry_ref)

    jj = lax.broadcasted_iota(jnp.int32, (PAGE_SIZE, PAGE_SIZE), 0)
    ss = lax.broadcasted_iota(jnp.int32, (PAGE_SIZE, PAGE_SIZE), 1)
    suffix_ones = (jj >= ss).astype(BF16)
    bias = bias_ref[...]
    carry = carry_ref[...]

    for p in range(n_pages_step):
        for h in range(N_HEADS):
            y_ref[h:h + 1, :] = jnp.sum(qb_ref[h] * k_refs[p][h], axis=0, keepdims=True)
        y = y_ref[...] + bias
        sp = _softplus2(y)
        c_local = jnp.dot(sp.astype(BF16), suffix_ones, preferred_element_type=F32)
        a_ref[p] = jnp.exp2(y - c_local - carry)
        carry = carry + c_local[:, 0:1]
    carry_ref[...] = carry

    for h in range(N_HEADS):
        acc = acc_ref[h]
        for p in range(n_pages_step):
            acc = acc + a_ref[p, h:h + 1, :] * v_refs[p][h]
        acc_ref[h] = acc

    @pl.when(step == pl.num_programs(1) - 1)
    def _():
        o_ref[0] = jnp.sum(acc_ref[...], axis=2, keepdims=True)


def _attn_decode(qs, cache_kt, cache_vt, layer, page_table, bias2):
    db, d = qs.shape
    dh = d // N_HEADS
    n_pages = page_table.shape[1]
    pps = min(DECODE_PAGES_PER_STEP, n_pages)
    assert n_pages % pps == 0

    def page_spec(p):
        def index(bi, st, pt):
            return (layer, pt[bi * n_pages + (n_pages - 1 - (st * pps + p))], 0, 0, 0)
        return pl.BlockSpec((None, None, N_HEADS, dh, PAGE_SIZE), index)

    specs = [page_spec(p) for p in range(pps)]
    out = pl.pallas_call(
        functools.partial(_attn_decode_kernel, n_pages_step=pps),
        grid_spec=pltpu.PrefetchScalarGridSpec(
            num_scalar_prefetch=1,
            grid=(db, n_pages // pps),
            in_specs=[pl.BlockSpec((1, N_HEADS, dh, 1), lambda bi, st, pt: (bi, 0, 0, 0)),
                      pl.BlockSpec((N_HEADS, 1), lambda bi, st, pt: (0, 0))] + specs + specs,
            out_specs=pl.BlockSpec((1, N_HEADS, dh, 1), lambda bi, st, pt: (bi, 0, 0, 0)),
            scratch_shapes=[pltpu.VMEM((N_HEADS, dh, PAGE_SIZE), F32),
                            pltpu.VMEM((N_HEADS, dh, PAGE_SIZE), F32),
                            pltpu.VMEM((N_HEADS, 1), F32),
                            pltpu.VMEM((N_HEADS, PAGE_SIZE), F32),
                            pltpu.VMEM((pps, N_HEADS, PAGE_SIZE), F32)]),
        out_shape=jax.ShapeDtypeStruct((db, N_HEADS, dh, 1), F32),
        compiler_params=_params("arbitrary", "arbitrary"),
        name="attn_decode",
    )(page_table.reshape(-1), qs.reshape(db, N_HEADS, dh, 1), bias2.reshape(N_HEADS, 1),
      *([cache_kt] * pps), *([cache_vt] * pps))
    return out.reshape(db, d)


def _pool_groups(window_sum, u_of, inv_cnt_of, wg_ref, scale):
    outs = []
    for gi, w in enumerate(POOL_WINDOWS):
        mixed = (window_sum(gi, w) * inv_cnt_of(w) - u_of(gi)).astype(BF16)
        outs.append(jnp.dot(mixed, wg_ref[gi], preferred_element_type=F32))
    return jnp.concatenate(outs, axis=-1) * scale


def _pool_prompt_kernel(u_ref, prev_ref, gate_ref, h_ref, wg_ref, scale_ref, w_ref, out_ref, ext_ref,
                        *, tm, grp):
    t = pl.program_id(1)
    prev = prev_ref[0]
    ext_ref[0:POOL_PREV_ROWS, :] = jnp.where(t > 0, prev, jnp.zeros_like(prev))
    ext_ref[POOL_PREV_ROWS:, :] = u_ref[0]
    pos = t * tm + lax.broadcasted_iota(jnp.int32, (tm, 1), 0)

    def window_sum(gi, w):
        sl = slice(gi * grp, (gi + 1) * grp)
        s = ext_ref[POOL_PREV_ROWS:POOL_PREV_ROWS + tm, sl]
        for i in range(1, w):
            s = s + ext_ref[POOL_PREV_ROWS - i:POOL_PREV_ROWS - i + tm, sl]
        return s

    def u_of(gi):
        return ext_ref[POOL_PREV_ROWS:POOL_PREV_ROWS + tm, gi * grp:(gi + 1) * grp]

    def inv_cnt_of(w):
        return 1.0 / jnp.minimum(w, pos + 1).astype(F32)

    mix = _pool_groups(window_sum, u_of, inv_cnt_of, wg_ref, scale_ref[...])
    out_ref[0] = _gate_out(mix, gate_ref[0], h_ref[0], w_ref)


def _pool_prompt(u, gate, h, wg_bf16, scale, w_out_bf16, tm_want):
    b, s, d = u.shape
    tm = _row_tile(s, tm_want)
    assert tm % POOL_PREV_ROWS == 0
    grp = d // len(POOL_WINDOWS)
    ratio = tm // POOL_PREV_ROWS
    row_spec = pl.BlockSpec((1, tm, d), lambda bi, t: (bi, t, 0))
    return pl.pallas_call(
        functools.partial(_pool_prompt_kernel, tm=tm, grp=grp),
        grid=(b, s // tm),
        in_specs=[row_spec,
                  pl.BlockSpec((1, POOL_PREV_ROWS, d), lambda bi, t: (bi, jnp.maximum(t * ratio - 1, 0), 0)),
                  row_spec, row_spec,
                  pl.BlockSpec((len(POOL_WINDOWS), grp, grp), lambda bi, t: (0, 0, 0)),
                  pl.BlockSpec((1, d), lambda bi, t: (0, 0)),
                  pl.BlockSpec((d, d), lambda bi, t: (0, 0))],
        out_specs=row_spec,
        out_shape=jax.ShapeDtypeStruct((b, s, d), F32),
        scratch_shapes=[pltpu.VMEM((tm + POOL_PREV_ROWS, d), F32)],
        compiler_params=_params("arbitrary", "arbitrary"),
        name="pool_prompt",
    )(u, u, gate, h, wg_bf16, scale.reshape(1, d), w_out_bf16)


def _pool_sample_kernel(hist_ref, u_ref, gate_ref, h_ref, wg_ref, scale_ref, w_ref, out_ref, *, grp, past):
    n_hist = hist_ref.shape[0]

    def window_sum(gi, w):
        sl = slice(gi * grp, (gi + 1) * grp)
        s = u_ref[:, sl]
        for i in range(1, w):
            s = s + hist_ref[n_hist - i, :, sl]
        return s

    def u_of(gi):
        return u_ref[:, gi * grp:(gi + 1) * grp]

    def inv_cnt_of(w):
        return 1.0 / float(min(w, past + 1))

    mix = _pool_groups(window_sum, u_of, inv_cnt_of, wg_ref, scale_ref[...])
    out_ref[...] = _gate_out(mix, gate_ref[...], h_ref[...], w_ref)


def _pool_sample(hist_t, u, gate, h, wg_bf16, scale, w_out_bf16, past):
    db, d = u.shape
    grp = d // len(POOL_WINDOWS)
    return pl.pallas_call(
        functools.partial(_pool_sample_kernel, grp=grp, past=past),
        out_shape=jax.ShapeDtypeStruct((db, d), F32),
        compiler_params=pltpu.CompilerParams(vmem_limit_bytes=VMEM_LIMIT_BYTES),
        name="pool_sample",
    )(hist_t, u, gate, h, wg_bf16, scale.reshape(1, d), w_out_bf16)


def _ln_silu(y, ln_g, ln_b):
    mu = jnp.mean(y, axis=-1, keepdims=True)
    yc = y - mu
    yn = yc * lax.rsqrt(jnp.mean(yc * yc, axis=-1, keepdims=True) + LN_EPS) * ln_g + ln_b
    return _silu(yn)


def _conv_prompt_kernel(glu_ref, prev_ref, gate_ref, h_ref, dw_ref, db_ref, lg_ref, lb_ref, w_ref,
                        out_ref, ext_ref, shifted_ref, mix_ref, *, tm):
    t = pl.program_id(1)
    prev = prev_ref[0]
    ext_ref[0:CONV_PREV_ROWS, :] = jnp.where(t > 0, prev, jnp.zeros_like(prev))
    ext_ref[CONV_PREV_ROWS:, :] = glu_ref[0]
    lead = CONV_PREV_ROWS - (CONV_WIDTH - 1)
    n_shifted = shifted_ref.shape[1]
    for shift in range(1, SUBLANES):
        shifted_ref[shift - 1] = ext_ref[shift:shift + n_shifted, :]
    for c in range(tm // CONV_ROW_CHUNK):
        r0 = c * CONV_ROW_CHUNK
        y = None
        for i in range(CONV_WIDTH):
            shift = (lead + i) % SUBLANES
            q = r0 + lead + i - shift
            if shift == 0:
                rows = ext_ref[q:q + CONV_ROW_CHUNK, :]
            else:
                rows = shifted_ref[shift - 1, q:q + CONV_ROW_CHUNK, :]
            term = dw_ref[i:i + 1, :] * rows
            y = term if y is None else y + term
        mix_ref[r0:r0 + CONV_ROW_CHUNK, :] = _ln_silu(y + db_ref[...], lg_ref[...], lb_ref[...])
    out_ref[0] = _gate_out(mix_ref[...], gate_ref[0], h_ref[0], w_ref)


def _conv_prompt(glu, gate, h, dw_w, dw_b, ln_g, ln_b, w_out_bf16, tm_want):
    b, s, d = glu.shape
    tm = _row_tile(s, tm_want)
    assert tm % CONV_PREV_ROWS == 0 and tm % CONV_ROW_CHUNK == 0
    ratio = tm // CONV_PREV_ROWS
    row_spec = pl.BlockSpec((1, tm, d), lambda bi, t: (bi, t, 0))
    vec_spec = pl.BlockSpec((1, d), lambda bi, t: (0, 0))
    return pl.pallas_call(
        functools.partial(_conv_prompt_kernel, tm=tm),
        grid=(b, s // tm),
        in_specs=[row_spec,
                  pl.BlockSpec((1, CONV_PREV_ROWS, d), lambda bi, t: (bi, jnp.maximum(t * ratio - 1, 0), 0)),
                  row_spec, row_spec,
                  pl.BlockSpec((CONV_WIDTH, d), lambda bi, t: (0, 0)),
                  vec_spec, vec_spec, vec_spec,
                  pl.BlockSpec((d, d), lambda bi, t: (0, 0))],
        out_specs=row_spec,
        out_shape=jax.ShapeDtypeStruct((b, s, d), F32),
        scratch_shapes=[pltpu.VMEM((tm + CONV_PREV_ROWS, d), F32),
                        pltpu.VMEM((SUBLANES - 1, tm + CONV_PREV_ROWS - SUBLANES, d), F32),
                        pltpu.VMEM((tm, d), F32)],
        compiler_params=_params("arbitrary", "arbitrary"),
        name="conv_prompt",
    )(glu, glu, gate, h, dw_w, dw_b.reshape(1, d), ln_g.reshape(1, d), ln_b.reshape(1, d), w_out_bf16)


def _conv_sample_kernel(hist_ref, glu_ref, gate_ref, h_ref, dw_ref, db_ref, lg_ref, lb_ref, w_ref, out_ref):
    n_hist = hist_ref.shape[0]
    y = dw_ref[n_hist:n_hist + 1, :] * glu_ref[...]
    for i in range(n_hist):
        y = y + dw_ref[i:i + 1, :] * hist_ref[i]
    mix = _ln_silu(y + db_ref[...], lg_ref[...], lb_ref[...])
    out_ref[...] = _gate_out(mix, gate_ref[...], h_ref[...], w_ref)


def _conv_sample(hist_t, glu, gate, h, dw_w, dw_b, ln_g, ln_b, w_out_bf16):
    db, d = glu.shape
    return pl.pallas_call(
        _conv_sample_kernel,
        out_shape=jax.ShapeDtypeStruct((db, d), F32),
        compiler_params=pltpu.CompilerParams(vmem_limit_bytes=VMEM_LIMIT_BYTES),
        name="conv_sample",
    )(hist_t, glu, gate, h, dw_w, dw_b.reshape(1, d), ln_g.reshape(1, d), ln_b.reshape(1, d), w_out_bf16)


def kernel(x_prompt, x_sample, cache_k, cache_v, state_pool, state_conv, page_table, norm_g, final_norm_g,
           attn_w_in, attn_bias, attn_w_out, pool_w_in, pool_w_group, pool_scale, pool_w_out, conv_w_in,
           conv_dw_w, conv_dw_b, conv_ln_g, conv_ln_b, conv_w_out):
    b, s, d = x_prompt.shape
    db, ds, _ = x_sample.shape
    assert ds == 1 and d % LANES == 0 and d // N_HEADS * HEADS_PER_BLOCK == LANES
    depth = norm_g.shape[0]
    past = page_table.shape[1] * PAGE_SIZE
    dh = d // N_HEADS
    pool_hist = state_pool.shape[2]
    conv_hist = state_conv.shape[2]
    assert pool_hist == max(POOL_WINDOWS) - 1 and conv_hist == CONV_WIDTH - 1

    cache_kt = jnp.transpose(cache_k, (0, 1, 3, 4, 2))
    cache_vt = jnp.transpose(cache_v, (0, 1, 3, 4, 2))

    def heads_last(xt):
        return jnp.transpose(xt.reshape(b, N_HEADS, dh, s), (0, 3, 1, 2))

    hp = x_prompt
    hs = x_sample.reshape(db, d)
    kp_l, vp_l, ks_l, vs_l = [], [], [], []
    poolp_l, pools_l, convp_l, convs_l = [], [], [], []
    for i in range(depth):
        kind = i % N_MIXERS
        j = i // N_MIXERS
        last = i == depth - 1
        if kind == 0:
            w_out = attn_w_out[j].astype(BF16)
            bias2 = attn_bias[j] * LOG2E
            qt, kt, kb, vt, vtb, gp = _inproj_attn_prompt(hp, norm_g[i], attn_w_in[j])
            qsb, ks_, vs, gs = _inproj(hs, norm_g[i], attn_w_in[j].astype(BF16), "attn", 256)
            op = _attn_prompt(qt, kb, vtb, bias2)
            os_ = _attn_decode(qsb.astype(F32), cache_kt, cache_vt, j, page_table, bias2)
            hp = _outproj(op.reshape(b * s, d), gp.reshape(b * s, d), hp.reshape(b * s, d), w_out,
                          final_norm_g, last, 512).reshape(b, s, d)
            hs = _outproj(os_, gs, hs, w_out, final_norm_g, last, 512)
            kp_l.append(heads_last(kt))
            vp_l.append(heads_last(vt))
            ks_l.append(ks_.reshape(db, ds, N_HEADS, dh))
            vs_l.append(vs.reshape(db, ds, N_HEADS, dh))
        elif kind == 1:
            w_in = pool_w_in[j].astype(BF16)
            w_out = pool_w_out[j].astype(BF16)
            wg = pool_w_group[j].astype(BF16)
            up, gp = _inproj(hp.reshape(b * s, d), norm_g[i], w_in, "pool", 512)
            us, gs = _inproj(hs, norm_g[i], w_in, "pool", 512)
            up3 = up.reshape(b, s, d)
            hp = _pool_prompt(up3, gp.reshape(b, s, d), hp, wg, pool_scale[j], w_out, 256)
            hist_t = jnp.transpose(state_pool[j], (1, 0, 2))
            hs = _pool_sample(hist_t, us, gs, hs, wg, pool_scale[j], w_out, past)
            if s >= pool_hist:
                poolp_l.append(up3[:, s - pool_hist:])
            else:
                poolp_l.append(jnp.concatenate([jnp.zeros((b, pool_hist - s, d), F32), up3], axis=1))
            pools_l.append(jnp.concatenate([state_pool[j][:, ds:], us.reshape(db, ds, d)], axis=1))
        else:
            w_in = conv_w_in[j].astype(BF16)
            w_out = conv_w_out[j].astype(BF16)
            glu_p, gp = _inproj(hp.reshape(b * s, d), norm_g[i], w_in, "conv", 512)
            glu_s, gs = _inproj(hs, norm_g[i], w_in, "conv", 512)
            glu3 = glu_p.reshape(b, s, d)
            hp = _conv_prompt(glu3, gp.reshape(b, s, d), hp, conv_dw_w[j], conv_dw_b[j],
                              conv_ln_g[j], conv_ln_b[j], w_out, 256)
            hist_t = jnp.transpose(state_conv[j], (1, 0, 2))
            hs = _conv_sample(hist_t, glu_s, gs, hs, conv_dw_w[j], conv_dw_b[j], conv_ln_g[j], conv_ln_b[j],
                              w_out)
            if s >= conv_hist:
                convp_l.append(glu3[:, s - conv_hist:])
            else:
                convp_l.append(jnp.concatenate([jnp.zeros((b, conv_hist - s, d), F32), glu3], axis=1))
            convs_l.append(jnp.concatenate([state_conv[j][:, ds:], glu_s.reshape(db, ds, d)], axis=1))
    assert depth % N_MIXERS == 1, "the final RMSNorm is fused into the attention output projection"
    return (hp, hs.reshape(db, ds, d), jnp.stack(kp_l), jnp.stack(vp_l), jnp.stack(ks_l),
            jnp.stack(vs_l), jnp.stack(poolp_l), jnp.stack(pools_l), jnp.stack(convp_l), jnp.stack(convs_l))
```

```python
import functools
import math

import jax
import jax.numpy as jnp
from jax import lax
from jax.experimental import pallas as pl
from jax.experimental.pallas import tpu as pltpu

F32 = jnp.float32
BF16 = jnp.bfloat16

N_MIXERS = 3
N_HEADS = 16
PAGE_SIZE = 128
POOL_WINDOWS = (2, 4, 8, 16)
CONV_WIDTH = 31
RMS_EPS = 1e-6
LN_EPS = 1e-5
LOG2E = math.log2(math.e)

LANES = 128
SUBLANES = 8
HEADS_PER_BLOCK = 2
VMEM_LIMIT_BYTES = 56 * 1024 * 1024
ATTN_BLOCK = 256
ATTN_GROUP = 6
ATTN_QBLOCKS_PER_STEP = 4
DECODE_PAGES_PER_STEP = 16
POOL_PREV_ROWS = 16
CONV_PREV_ROWS = 32
CONV_ROW_CHUNK = 32

_NT = (((1,), (1,)), ((), ()))


def _row_tile(rows, want):
    tm = min(rows, want)
    assert rows % tm == 0 and tm % 8 == 0, (rows, tm)
    return tm


def _params(*sem):
    return pltpu.CompilerParams(dimension_semantics=sem, vmem_limit_bytes=VMEM_LIMIT_BYTES)


def _sigmoid(x):
    return 1.0 / (1.0 + jnp.exp(-x))


def _silu(x):
    return x * _sigmoid(x)


def _softplus2(y):
    neg_abs = lax.bitcast_convert_type(lax.bitcast_convert_type(y, jnp.int32) | jnp.int32(-2 ** 31), F32)
    return jnp.maximum(y, 0.0) + jnp.log(1.0 + jnp.exp2(neg_abs)) * LOG2E


def _rmsnorm(x, g):
    return x * lax.rsqrt(jnp.mean(x * x, axis=-1, keepdims=True) + RMS_EPS) * g


def _inproj_kernel(x_ref, g_ref, w_ref, *out_refs, mode, d, q_scale):
    n = _rmsnorm(x_ref[...], g_ref[...]).astype(BF16)

    def chunk(c):
        return jnp.dot(n, w_ref[:, c * d:(c + 1) * d], preferred_element_type=F32)

    if mode == "attn":
        qb_ref, k_ref, v_ref, gate_ref = out_refs
        qb_ref[...] = (chunk(0) * q_scale).astype(BF16)
        k_ref[...] = chunk(1)
        v_ref[...] = chunk(2)
        gate_ref[...] = chunk(3)
    elif mode == "pool":
        u_ref, gate_ref = out_refs
        u_ref[...] = chunk(0)
        gate_ref[...] = chunk(1)
    else:
        glu_ref, gate_ref = out_refs
        glu_ref[...] = chunk(0) * _sigmoid(chunk(1))
        gate_ref[...] = chunk(2)


def _q_scale(d):
    return (d // N_HEADS) ** -0.5 * LOG2E


def _inproj(x, norm_g, w_bf16, mode, tm_want):
    rows, d = x.shape
    n_out = w_bf16.shape[1]
    tm = _row_tile(rows, tm_want)
    row_spec = pl.BlockSpec((tm, d), lambda i: (i, 0))
    dtypes = (BF16, F32, F32, F32) if mode == "attn" else (F32, F32)
    return pl.pallas_call(
        functools.partial(_inproj_kernel, mode=mode, d=d, q_scale=_q_scale(d)),
        grid=(rows // tm,),
        in_specs=[row_spec,
                  pl.BlockSpec((1, d), lambda i: (0, 0)),
                  pl.BlockSpec((d, n_out), lambda i: (0, 0))],
        out_specs=[row_spec] * len(dtypes),
        out_shape=[jax.ShapeDtypeStruct((rows, d), t) for t in dtypes],
        compiler_params=_params("arbitrary"),
        name=f"inproj_{mode}",
    )(x, norm_g.reshape(1, d), w_bf16)


def _inproj_attn_prompt_kernel(x_ref, g_ref, wt_ref, wg_ref, *refs, d, q_scale):
    qt_ref, kt_ref, kb_ref, vt_ref, vtb_ref, gate_ref = refs[-6:]
    n = _rmsnorm(x_ref[0], g_ref[...]).astype(BF16)

    def chunk_t(c):
        return lax.dot_general(wt_ref[c * d:(c + 1) * d, :], n, _NT, preferred_element_type=F32)

    qt_ref[0, 0] = (chunk_t(0) * q_scale).astype(BF16)
    kt = chunk_t(1)
    kt_ref[0, 0] = kt
    kb_ref[0] = jnp.transpose(kt).astype(BF16)
    vt = chunk_t(2)
    vt_ref[0, 0] = vt
    vtb_ref[0, 0] = vt.astype(BF16)
    gate_ref[0] = jnp.dot(n, wg_ref[...], preferred_element_type=F32)


def _inproj_attn_prompt(x, norm_g, w_in, layer, n_layers, kv_all):
    b, s, d = x.shape
    blk = min(ATTN_BLOCK, s)
    assert s % blk == 0
    nt = s // blk
    wt = jnp.transpose(w_in[:, :3 * d]).astype(BF16)
    wg = w_in[:, 3 * d:].astype(BF16)
    row_spec = pl.BlockSpec((1, blk, d), lambda bi, t: (bi, t, 0))
    col_spec = pl.BlockSpec((1, 1, d, blk), lambda bi, t: (layer, bi, 0, t))
    tile_spec = pl.BlockSpec((1, 1, d, blk), lambda bi, t: (bi, t, 0, 0))
    const = lambda shape: pl.BlockSpec(shape, lambda bi, t: (0,) * len(shape))
    in_specs = [row_spec, const((1, d)), const((3 * d, d)), const((d, d))]
    operands = [x, norm_g.reshape(1, d), wt, wg]
    aliases = {}
    if kv_all is not None:
        in_specs += [pl.BlockSpec(memory_space=pl.ANY)] * 2
        aliases = {len(operands): 1, len(operands) + 1: 3}
        operands += list(kv_all)
    return pl.pallas_call(
        functools.partial(_inproj_attn_prompt_kernel, d=d, q_scale=_q_scale(d)),
        grid=(b, nt),
        in_specs=in_specs,
        out_specs=[tile_spec, col_spec, row_spec, col_spec, tile_spec, row_spec],
        out_shape=[jax.ShapeDtypeStruct((b, nt, d, blk), BF16),
                   jax.ShapeDtypeStruct((n_layers, b, d, s), F32),
                   jax.ShapeDtypeStruct((b, s, d), BF16),
                   jax.ShapeDtypeStruct((n_layers, b, d, s), F32),
                   jax.ShapeDtypeStruct((b, nt, d, blk), BF16),
                   jax.ShapeDtypeStruct((b, s, d), F32)],
        input_output_aliases=aliases,
        compiler_params=_params("arbitrary", "arbitrary"),
        name="inproj_attn_prompt",
    )(*operands)


def _gate_out(mix, gate, h, w_out_ref):
    act = (mix * _silu(gate)).astype(BF16)
    return h + jnp.dot(act, w_out_ref[...], preferred_element_type=F32)


def _outproj_kernel(o_ref, gate_ref, h_ref, w_ref, fg_ref, out_ref, *, final_norm):
    h_new = _gate_out(o_ref[...], gate_ref[...], h_ref[...], w_ref)
    if final_norm:
        h_new = _rmsnorm(h_new, fg_ref[...])
    out_ref[...] = h_new


def _outproj(o, gate, h, w_out_bf16, final_g, final_norm, tm_want):
    rows, d = h.shape
    tm = _row_tile(rows, tm_want)
    row_spec = pl.BlockSpec((tm, d), lambda i: (i, 0))
    return pl.pallas_call(
        functools.partial(_outproj_kernel, final_norm=final_norm),
        grid=(rows // tm,),
        in_specs=[row_spec, row_spec, row_spec,
                  pl.BlockSpec((d, d), lambda i: (0, 0)),
                  pl.BlockSpec((1, d), lambda i: (0, 0))],
        out_specs=row_spec,
        out_shape=jax.ShapeDtypeStruct((rows, d), F32),
        compiler_params=_params("arbitrary"),
        name="outproj_final" if final_norm else "outproj",
    )(o, gate, h, w_out_bf16, final_g.reshape(1, d))


def _attn_prompt_kernel(bias_ref, qt_ref, k_ref, vt_ref, o_ref, acc_ref, carry_ref, y_buf,
                        *, blk, n_heads, n_qblocks):
    hp = pl.program_id(1)
    grp = ATTN_GROUP
    half = LANES // HEADS_PER_BLOCK
    chan = lax.broadcasted_iota(jnp.int32, (LANES, 1), 0)
    first_head = chan < half

    col2 = lax.broadcasted_iota(jnp.int32, (1, 2 * blk), 1)

    def bias_piece(p):
        return jnp.where(col2 < blk, bias_ref[p * n_heads + 2 * hp], bias_ref[p * n_heads + 2 * hp + 1])

    bias_rows = jnp.where(chan == 0, bias_piece(0),
                          jnp.where(chan == 1, bias_piece(1),
                                    jnp.where(chan == 2, bias_piece(2), 0.0))).astype(BF16)
    lane = lax.broadcasted_iota(jnp.int32, (blk, LANES), 1)
    ones_cols = (lane < 3).astype(BF16)

    ss = lax.broadcasted_iota(jnp.int32, (blk, blk), 0)
    jj = lax.broadcasted_iota(jnp.int32, (blk, blk), 1)
    suffix_ones = (jj >= ss).astype(BF16)

    def per_query_block(j, _):
        qi = pl.program_id(2) * n_qblocks + j
        qt = qt_ref[0, j]
        zero = jnp.zeros_like(qt)
        qq = jnp.concatenate([jnp.where(first_head, qt, zero), jnp.where(first_head, zero, qt)], axis=1)
        qq_ext = jnp.concatenate([qq, bias_rows], axis=0)
        acc_ref[...] = jnp.zeros_like(acc_ref)
        carry_ref[...] = jnp.zeros_like(carry_ref)

        def tiles(kbs, diagonal):
            n = len(kbs)
            for i, kb in enumerate(kbs):
                ks = k_ref[0, pl.ds(pl.multiple_of(kb * blk, blk), blk), :]
                ks_ext = jnp.concatenate([ks, ones_cols], axis=1)
                y_buf[i] = jnp.dot(ks_ext, qq_ext, preferred_element_type=F32)
            if any(diagonal):
                r = lax.broadcasted_iota(jnp.int32, (blk, 2 * blk), 0)
                c = lax.broadcasted_iota(jnp.int32, (blk, 2 * blk), 1)
                visible = r < jnp.where(c < blk, c, c - blk)
            parts = []

            def second_half(i, c_local):
                a = jnp.exp2(y_buf[i] - c_local)
                if diagonal[i]:
                    a = jnp.where(visible, a, 0.0)
                ab = a.astype(BF16)
                vt = vt_ref[0, kbs[i]]
                p0 = jnp.dot(vt[:half], ab[:, :blk], preferred_element_type=F32)
                p1 = jnp.dot(vt[half:], ab[:, blk:], preferred_element_type=F32)
                parts.append((p0, p1, c_local[0:1, :]))

            pending = None
            for i in range(n):
                sp = _softplus2(y_buf[i])
                if diagonal[i]:
                    sp = jnp.where(visible, sp, 0.0)
                c_local = jnp.dot(suffix_ones, sp.astype(BF16), preferred_element_type=F32)
                if pending is not None:
                    second_half(*pending)
                pending = (i, c_local)
            second_half(*pending)
            return parts

        def accumulate(parts):
            for p0, p1, total in parts:
                carry = carry_ref[...]
                decay = jnp.exp2(-carry)
                acc_ref[0:half, :] += p0 * decay[:, :blk]
                acc_ref[half:, :] += p1 * decay[:, blk:]
                carry_ref[...] = carry + total

        for lead in range(grp):
            @pl.when(qi % grp == lead)
            def _(lead=lead):
                accumulate(tiles([qi - i for i in range(lead + 1)], [True] + [False] * lead))

        def group(g, _):
            top = qi - qi % grp - 1 - grp * g
            accumulate(tiles([top - i for i in range(grp)], [False] * grp))
            return 0

        lax.fori_loop(0, qi // grp, group, 0)

        o_ref[0, pl.ds(pl.multiple_of(j * blk, blk), blk), :] = jnp.transpose(acc_ref[...])
        return 0

    lax.fori_loop(0, n_qblocks, per_query_block, 0)


def _attn_prompt(qt, kb, vtb, bias2):
    b, nt, d, blk = qt.shape
    s = nt * blk
    n_pairs = d // LANES
    hi = bias2.astype(BF16).astype(F32)
    mid = (bias2 - hi).astype(BF16).astype(F32)
    lo = (bias2 - hi - mid).astype(BF16).astype(F32)
    pieces = jnp.concatenate([hi, mid, lo])
    nq = math.gcd(ATTN_QBLOCKS_PER_STEP, nt)
    return pl.pallas_call(
        functools.partial(_attn_prompt_kernel, blk=blk, n_heads=bias2.shape[0], n_qblocks=nq),
        grid_spec=pltpu.PrefetchScalarGridSpec(
            num_scalar_prefetch=1,
            grid=(b, n_pairs, nt // nq),
            in_specs=[pl.BlockSpec((1, nq, LANES, blk), lambda bi, hp, qs, bias: (bi, qs, hp, 0)),
                      pl.BlockSpec((1, s, LANES), lambda bi, hp, qs, bias: (bi, 0, hp)),
                      pl.BlockSpec((1, nt, LANES, blk), lambda bi, hp, qs, bias: (bi, 0, hp, 0))],
            out_specs=pl.BlockSpec((1, nq * blk, LANES), lambda bi, hp, qs, bias: (bi, qs, hp)),
            scratch_shapes=[pltpu.VMEM((LANES, blk), F32), pltpu.VMEM((1, 2 * blk), F32),
                            pltpu.VMEM((ATTN_GROUP, blk, 2 * blk), F32)]),
        out_shape=jax.ShapeDtypeStruct((b, s, d), F32),
        compiler_params=pltpu.CompilerParams(
            dimension_semantics=("arbitrary", "arbitrary", "arbitrary"), vmem_limit_bytes=VMEM_LIMIT_BYTES,
        ),
        name="attn_prompt",
    )(pieces, qt, kb, vtb)


def _attn_decode_kernel(pt_ref, q_ref, bias_ref, *refs, n_pages_step):
    k_refs = refs[:n_pages_step]
    v_refs = refs[n_pages_step:2 * n_pages_step]
    o_ref, qb_ref, acc_ref, carry_ref, y_ref, a_ref = refs[2 * n_pages_step:]
    step = pl.program_id(1)

    @pl.when(step == 0)
    def _():
        qb_ref[...] = jnp.broadcast_to(q_ref[0], qb_ref.shape)
        acc_ref[...] = jnp.zeros_like(acc_ref)
        carry_ref[...] = jnp.zeros_like(carry_ref)

    jj = lax.broadcasted_iota(jnp.int32, (PAGE_SIZE, PAGE_SIZE), 0)
    ss = lax.broadcasted_iota(jnp.int32, (PAGE_SIZE, PAGE_SIZE), 1)
    suffix_ones = (jj >= ss).astype(BF16)
    bias = bias_ref[...]
    carry = carry_ref[...]

    for p in range(n_pages_step):
        for h in range(N_HEADS):
            y_ref[h:h + 1, :] = jnp.sum(qb_ref[h] * k_refs[p][h], axis=0, keepdims=True)
        y = y_ref[...] + bias
        sp = _softplus2(y)
        c_local = jnp.dot(sp.astype(BF16), suffix_ones, preferred_element_type=F32)
        a_ref[p] = jnp.exp2(y - c_local - carry)
        carry = carry + c_local[:, 0:1]
    carry_ref[...] = carry

    for h in range(N_HEADS):
        acc = acc_ref[h]
        for p in range(n_pages_step):
            acc = acc + a_ref[p, h:h + 1, :] * v_refs[p][h]
        acc_ref[h] = acc

    @pl.when(step == pl.num_programs(1) - 1)
    def _():
        o_ref[0] = jnp.sum(acc_ref[...], axis=2, keepdims=True)


def _attn_decode(qs, cache_kt, cache_vt, layer, page_table, bias2):
    db, d = qs.shape
    dh = d // N_HEADS
    n_pages = page_table.shape[1]
    pps = min(DECODE_PAGES_PER_STEP, n_pages)
    assert n_pages % pps == 0

    def page_spec(p):
        def index(bi, st, pt):
            return (layer, pt[bi * n_pages + (n_pages - 1 - (st * pps + p))], 0, 0, 0)
        return pl.BlockSpec((None, None, N_HEADS, dh, PAGE_SIZE), index)

    specs = [page_spec(p) for p in range(pps)]
    out = pl.pallas_call(
        functools.partial(_attn_decode_kernel, n_pages_step=pps),
        grid_spec=pltpu.PrefetchScalarGridSpec(
            num_scalar_prefetch=1,
            grid=(db, n_pages // pps),
            in_specs=[pl.BlockSpec((1, N_HEADS, dh, 1), lambda bi, st, pt: (bi, 0, 0, 0)),
                      pl.BlockSpec((N_HEADS, 1), lambda bi, st, pt: (0, 0))] + specs + specs,
            out_specs=pl.BlockSpec((1, N_HEADS, dh, 1), lambda bi, st, pt: (bi, 0, 0, 0)),
            scratch_shapes=[pltpu.VMEM((N_HEADS, dh, PAGE_SIZE), F32),
                            pltpu.VMEM((N_HEADS, dh, PAGE_SIZE), F32),
                            pltpu.VMEM((N_HEADS, 1), F32),
                            pltpu.VMEM((N_HEADS, PAGE_SIZE), F32),
                            pltpu.VMEM((pps, N_HEADS, PAGE_SIZE), F32)]),
        out_shape=jax.ShapeDtypeStruct((db, N_HEADS, dh, 1), F32),
        compiler_params=_params("arbitrary", "arbitrary"),
        name="attn_decode",
    )(page_table.reshape(-1), qs.reshape(db, N_HEADS, dh, 1), bias2.reshape(N_HEADS, 1),
      *([cache_kt] * pps), *([cache_vt] * pps))
    return out.reshape(db, d)


def _pool_groups(window_sum, u_of, inv_cnt_of, wg_ref, scale):
    outs = []
    for gi, w in enumerate(POOL_WINDOWS):
        mixed = (window_sum(gi, w) * inv_cnt_of(w) - u_of(gi)).astype(BF16)
        outs.append(jnp.dot(mixed, wg_ref[gi], preferred_element_type=F32))
    return jnp.concatenate(outs, axis=-1) * scale


def _pool_prompt_kernel(u_ref, prev_ref, gate_ref, h_ref, wg_ref, scale_ref, w_ref, out_ref, ext_ref,
                        *, tm, grp):
    t = pl.program_id(1)
    prev = prev_ref[0]
    ext_ref[0:POOL_PREV_ROWS, :] = jnp.where(t > 0, prev, jnp.zeros_like(prev))
    ext_ref[POOL_PREV_ROWS:, :] = u_ref[0]
    pos = t * tm + lax.broadcasted_iota(jnp.int32, (tm, 1), 0)

    def window_sum(gi, w):
        sl = slice(gi * grp, (gi + 1) * grp)
        s = ext_ref[POOL_PREV_ROWS:POOL_PREV_ROWS + tm, sl]
        for i in range(1, w):
            s = s + ext_ref[POOL_PREV_ROWS - i:POOL_PREV_ROWS - i + tm, sl]
        return s

    def u_of(gi):
        return ext_ref[POOL_PREV_ROWS:POOL_PREV_ROWS + tm, gi * grp:(gi + 1) * grp]

    def inv_cnt_of(w):
        return 1.0 / jnp.minimum(w, pos + 1).astype(F32)

    mix = _pool_groups(window_sum, u_of, inv_cnt_of, wg_ref, scale_ref[...])
    out_ref[0] = _gate_out(mix, gate_ref[0], h_ref[0], w_ref)


def _pool_prompt(u, gate, h, wg_bf16, scale, w_out_bf16, tm_want):
    b, s, d = u.shape
    tm = _row_tile(s, tm_want)
    assert tm % POOL_PREV_ROWS == 0
    grp = d // len(POOL_WINDOWS)
    ratio = tm // POOL_PREV_ROWS
    row_spec = pl.BlockSpec((1, tm, d), lambda bi, t: (bi, t, 0))
    return pl.pallas_call(
        functools.partial(_pool_prompt_kernel, tm=tm, grp=grp),
        grid=(b, s // tm),
        in_specs=[row_spec,
                  pl.BlockSpec((1, POOL_PREV_ROWS, d), lambda bi, t: (bi, jnp.maximum(t * ratio - 1, 0), 0)),
                  row_spec, row_spec,
                  pl.BlockSpec((len(POOL_WINDOWS), grp, grp), lambda bi, t: (0, 0, 0)),
                  pl.BlockSpec((1, d), lambda bi, t: (0, 0)),
                  pl.BlockSpec((d, d), lambda bi, t: (0, 0))],
        out_specs=row_spec,
        out_shape=jax.ShapeDtypeStruct((b, s, d), F32),
        scratch_shapes=[pltpu.VMEM((tm + POOL_PREV_ROWS, d), F32)],
        compiler_params=_params("arbitrary", "arbitrary"),
        name="pool_prompt",
    )(u, u, gate, h, wg_bf16, scale.reshape(1, d), w_out_bf16)


def _pool_sample_kernel(hist_ref, u_ref, gate_ref, h_ref, wg_ref, scale_ref, w_ref, out_ref, *, grp, past):
    n_hist = hist_ref.shape[0]

    def window_sum(gi, w):
        sl = slice(gi * grp, (gi + 1) * grp)
        s = u_ref[:, sl]
        for i in range(1, w):
            s = s + hist_ref[n_hist - i, :, sl]
        return s

    def u_of(gi):
        return u_ref[:, gi * grp:(gi + 1) * grp]

    def inv_cnt_of(w):
        return 1.0 / float(min(w, past + 1))

    mix = _pool_groups(window_sum, u_of, inv_cnt_of, wg_ref, scale_ref[...])
    out_ref[...] = _gate_out(mix, gate_ref[...], h_ref[...], w_ref)


def _pool_sample(hist_t, u, gate, h, wg_bf16, scale, w_out_bf16, past):
    db, d = u.shape
    grp = d // len(POOL_WINDOWS)
    return pl.pallas_call(
        functools.partial(_pool_sample_kernel, grp=grp, past=past),
        out_shape=jax.ShapeDtypeStruct((db, d), F32),
        compiler_params=pltpu.CompilerParams(vmem_limit_bytes=VMEM_LIMIT_BYTES),
        name="pool_sample",
    )(hist_t, u, gate, h, wg_bf16, scale.reshape(1, d), w_out_bf16)


def _ln_silu(y, ln_g, ln_b):
    mu = jnp.mean(y, axis=-1, keepdims=True)
    yc = y - mu
    yn = yc * lax.rsqrt(jnp.mean(yc * yc, axis=-1, keepdims=True) + LN_EPS) * ln_g + ln_b
    return _silu(yn)


def _conv_prompt_kernel(glu_ref, prev_ref, gate_ref, h_ref, dw_ref, db_ref, lg_ref, lb_ref, w_ref,
                        out_ref, ext_ref, shifted_ref, mix_ref, *, tm):
    t = pl.program_id(1)
    prev = prev_ref[0]
    ext_ref[0:CONV_PREV_ROWS, :] = jnp.where(t > 0, prev, jnp.zeros_like(prev))
    ext_ref[CONV_PREV_ROWS:, :] = glu_ref[0]
    lead = CONV_PREV_ROWS - (CONV_WIDTH - 1)
    n_shifted = shifted_ref.shape[1]
    for shift in range(1, SUBLANES):
        shifted_ref[shift - 1] = ext_ref[shift:shift + n_shifted, :]
    for c in range(tm // CONV_ROW_CHUNK):
        r0 = c * CONV_ROW_CHUNK
        y = None
        for i in range(CONV_WIDTH):
            shift = (lead + i) % SUBLANES
            q = r0 + lead + i - shift
            if shift == 0:
                rows = ext_ref[q:q + CONV_ROW_CHUNK, :]
            else:
                rows = shifted_ref[shift - 1, q:q + CONV_ROW_CHUNK, :]
            term = dw_ref[i:i + 1, :] * rows
            y = term if y is None else y + term
        mix_ref[r0:r0 + CONV_ROW_CHUNK, :] = _ln_silu(y + db_ref[...], lg_ref[...], lb_ref[...])
    out_ref[0] = _gate_out(mix_ref[...], gate_ref[0], h_ref[0], w_ref)


def _conv_prompt(glu, gate, h, dw_w, dw_b, ln_g, ln_b, w_out_bf16, tm_want):
    b, s, d = glu.shape
    tm = _row_tile(s, tm_want)
    assert tm % CONV_PREV_ROWS == 0 and tm % CONV_ROW_CHUNK == 0
    ratio = tm // CONV_PREV_ROWS
    row_spec = pl.BlockSpec((1, tm, d), lambda bi, t: (bi, t, 0))
    vec_spec = pl.BlockSpec((1, d), lambda bi, t: (0, 0))
    return pl.pallas_call(
        functools.partial(_conv_prompt_kernel, tm=tm),
        grid=(b, s // tm),
        in_specs=[row_spec,
                  pl.BlockSpec((1, CONV_PREV_ROWS, d), lambda bi, t: (bi, jnp.maximum(t * ratio - 1, 0), 0)),
                  row_spec, row_spec,
                  pl.BlockSpec((CONV_WIDTH, d), lambda bi, t: (0, 0)),
                  vec_spec, vec_spec, vec_spec,
                  pl.BlockSpec((d, d), lambda bi, t: (0, 0))],
        out_specs=row_spec,
        out_shape=jax.ShapeDtypeStruct((b, s, d), F32),
        scratch_shapes=[pltpu.VMEM((tm + CONV_PREV_ROWS, d), F32),
                        pltpu.VMEM((SUBLANES - 1, tm + CONV_PREV_ROWS - SUBLANES, d), F32),
                        pltpu.VMEM((tm, d), F32)],
        compiler_params=_params("arbitrary", "arbitrary"),
        name="conv_prompt",
    )(glu, glu, gate, h, dw_w, dw_b.reshape(1, d), ln_g.reshape(1, d), ln_b.reshape(1, d), w_out_bf16)


def _conv_sample_kernel(hist_ref, glu_ref, gate_ref, h_ref, dw_ref, db_ref, lg_ref, lb_ref, w_ref, out_ref):
    n_hist = hist_ref.shape[0]
    y = dw_ref[n_hist:n_hist + 1, :] * glu_ref[...]
    for i in range(n_hist):
        y = y + dw_ref[i:i + 1, :] * hist_ref[i]
    mix = _ln_silu(y + db_ref[...], lg_ref[...], lb_ref[...])
    out_ref[...] = _gate_out(mix, gate_ref[...], h_ref[...], w_ref)


def _conv_sample(hist_t, glu, gate, h, dw_w, dw_b, ln_g, ln_b, w_out_bf16):
    db, d = glu.shape
    return pl.pallas_call(
        _conv_sample_kernel,
        out_shape=jax.ShapeDtypeStruct((db, d), F32),
        compiler_params=pltpu.CompilerParams(vmem_limit_bytes=VMEM_LIMIT_BYTES),
        name="conv_sample",
    )(hist_t, glu, gate, h, dw_w, dw_b.reshape(1, d), ln_g.reshape(1, d), ln_b.reshape(1, d), w_out_bf16)


def kernel(x_prompt, x_sample, cache_k, cache_v, state_pool, state_conv, page_table, norm_g, final_norm_g,
           attn_w_in, attn_bias, attn_w_out, pool_w_in, pool_w_group, pool_scale, pool_w_out, conv_w_in,
           conv_dw_w, conv_dw_b, conv_ln_g, conv_ln_b, conv_w_out):
    b, s, d = x_prompt.shape
    db, ds, _ = x_sample.shape
    assert ds == 1 and d % LANES == 0 and d // N_HEADS * HEADS_PER_BLOCK == LANES
    depth = norm_g.shape[0]
    past = page_table.shape[1] * PAGE_SIZE
    dh = d // N_HEADS
    pool_hist = state_pool.shape[2]
    conv_hist = state_conv.shape[2]
    assert pool_hist == max(POOL_WINDOWS) - 1 and conv_hist == CONV_WIDTH - 1

    cache_kt = jnp.transpose(cache_k, (0, 1, 3, 4, 2))
    cache_vt = jnp.transpose(cache_v, (0, 1, 3, 4, 2))

    n_attn = len(range(0, depth, N_MIXERS))

    def heads_last(xt):
        return jnp.transpose(xt.reshape(n_attn, b, N_HEADS, dh, s), (0, 1, 4, 2, 3))

    hp = x_prompt
    hs = x_sample.reshape(db, d)
    kv_all = None
    ks_l, vs_l = [], []
    poolp_l, pools_l, convp_l, convs_l = [], [], [], []
    for i in range(depth):
        kind = i % N_MIXERS
        j = i // N_MIXERS
        last = i == depth - 1
        if kind == 0:
            w_out = attn_w_out[j].astype(BF16)
            bias2 = attn_bias[j] * LOG2E
            qt, kt_all, kb, vt_all, vtb, gp = _inproj_attn_prompt(hp, norm_g[i], attn_w_in[j], j, n_attn, kv_all)
            kv_all = (kt_all, vt_all)
            qsb, ks_, vs, gs = _inproj(hs, norm_g[i], attn_w_in[j].astype(BF16), "attn", 256)
            op = _attn_prompt(qt, kb, vtb, bias2)
            os_ = _attn_decode(qsb.astype(F32), cache_kt, cache_vt, j, page_table, bias2)
            hp = _outproj(op.reshape(b * s, d), gp.reshape(b * s, d), hp.reshape(b * s, d), w_out,
                          final_norm_g, last, 512).reshape(b, s, d)
            hs = _outproj(os_, gs, hs, w_out, final_norm_g, last, 512)
            ks_l.append(ks_.reshape(db, ds, N_HEADS, dh))
            vs_l.append(vs.reshape(db, ds, N_HEADS, dh))
        elif kind == 1:
            w_in = pool_w_in[j].astype(BF16)
            w_out = pool_w_out[j].astype(BF16)
            wg = pool_w_group[j].astype(BF16)
            up, gp = _inproj(hp.reshape(b * s, d), norm_g[i], w_in, "pool", 512)
            us, gs = _inproj(hs, norm_g[i], w_in, "pool", 512)
            up3 = up.reshape(b, s, d)
            hp = _pool_prompt(up3, gp.reshape(b, s, d), hp, wg, pool_scale[j], w_out, 256)
            hist_t = jnp.transpose(state_pool[j], (1, 0, 2))
            hs = _pool_sample(hist_t, us, gs, hs, wg, pool_scale[j], w_out, past)
            if s >= pool_hist:
                poolp_l.append(up3[:, s - pool_hist:])
            else:
                poolp_l.append(jnp.concatenate([jnp.zeros((b, pool_hist - s, d), F32), up3], axis=1))
            pools_l.append(jnp.concatenate([state_pool[j][:, ds:], us.reshape(db, ds, d)], axis=1))
        else:
            w_in = conv_w_in[j].astype(BF16)
            w_out = conv_w_out[j].astype(BF16)
            glu_p, gp = _inproj(hp.reshape(b * s, d), norm_g[i], w_in, "conv", 512)
            glu_s, gs = _inproj(hs, norm_g[i], w_in, "conv", 512)
            glu3 = glu_p.reshape(b, s, d)
            hp = _conv_prompt(glu3, gp.reshape(b, s, d), hp, conv_dw_w[j], conv_dw_b[j],
                              conv_ln_g[j], conv_ln_b[j], w_out, 256)
            hist_t = jnp.transpose(state_conv[j], (1, 0, 2))
            hs = _conv_sample(hist_t, glu_s, gs, hs, conv_dw_w[j], conv_dw_b[j], conv_ln_g[j], conv_ln_b[j],
                              w_out)
            if s >= conv_hist:
                convp_l.append(glu3[:, s - conv_hist:])
            else:
                convp_l.append(jnp.concatenate([jnp.zeros((b, conv_hist - s, d), F32), glu3], axis=1))
            convs_l.append(jnp.concatenate([state_conv[j][:, ds:], glu_s.reshape(db, ds, d)], axis=1))
    assert depth % N_MIXERS == 1, "the final RMSNorm is fused into the attention output projection"
    return (hp, hs.reshape(db, ds, d), heads_last(kv_all[0]), heads_last(kv_all[1]), jnp.stack(ks_l),
            jnp.stack(vs_l), jnp.stack(poolp_l), jnp.stack(pools_l), jnp.stack(convp_l), jnp.stack(convs_l))
```

```python
import functools
import math

import jax
import jax.numpy as jnp
from jax import lax
from jax.experimental import pallas as pl
from jax.experimental.pallas import tpu as pltpu

F32 = jnp.float32
BF16 = jnp.bfloat16

N_MIXERS = 3
N_HEADS = 16
PAGE_SIZE = 128
POOL_WINDOWS = (2, 4, 8, 16)
CONV_WIDTH = 31
RMS_EPS = 1e-6
LN_EPS = 1e-5
LOG2E = math.log2(math.e)

LANES = 128
SUBLANES = 8
HEADS_PER_BLOCK = 2
VMEM_LIMIT_BYTES = 56 * 1024 * 1024
ATTN_BLOCK = 256
ATTN_GROUP = 6
ATTN_QBLOCKS_PER_STEP = 4
DECODE_PAGES_PER_STEP = 16
POOL_PREV_ROWS = 16
CONV_PREV_ROWS = 32
CONV_ROW_CHUNK = 32

_NT = (((1,), (1,)), ((), ()))


def _row_tile(rows, want):
    tm = min(rows, want)
    assert rows % tm == 0 and tm % 8 == 0, (rows, tm)
    return tm


def _params(*sem):
    return pltpu.CompilerParams(dimension_semantics=sem, vmem_limit_bytes=VMEM_LIMIT_BYTES)


def _sigmoid(x):
    return 1.0 / (1.0 + jnp.exp(-x))


def _silu(x):
    return x * _sigmoid(x)


def _softplus2(y):
    neg_abs = lax.bitcast_convert_type(lax.bitcast_convert_type(y, jnp.int32) | jnp.int32(-2 ** 31), F32)
    return jnp.maximum(y, 0.0) + jnp.log(1.0 + jnp.exp2(neg_abs)) * LOG2E


def _rmsnorm(x, g):
    return x * lax.rsqrt(jnp.mean(x * x, axis=-1, keepdims=True) + RMS_EPS) * g


def _inproj_kernel(x_ref, g_ref, w_ref, *out_refs, mode, d, q_scale):
    n = _rmsnorm(x_ref[...], g_ref[...]).astype(BF16)

    def chunk(c):
        return jnp.dot(n, w_ref[:, c * d:(c + 1) * d], preferred_element_type=F32)

    if mode == "attn":
        qb_ref, k_ref, v_ref, gate_ref = out_refs
        qb_ref[...] = (chunk(0) * q_scale).astype(BF16)
        k_ref[...] = chunk(1)
        v_ref[...] = chunk(2)
        gate_ref[...] = chunk(3)
    elif mode == "pool":
        u_ref, gate_ref = out_refs
        u_ref[...] = chunk(0)
        gate_ref[...] = chunk(1)
    else:
        glu_ref, gate_ref = out_refs
        glu_ref[...] = chunk(0) * _sigmoid(chunk(1))
        gate_ref[...] = chunk(2)


def _q_scale(d):
    return (d // N_HEADS) ** -0.5 * LOG2E


def _inproj(x, norm_g, w_bf16, mode, tm_want):
    rows, d = x.shape
    n_out = w_bf16.shape[1]
    tm = _row_tile(rows, tm_want)
    row_spec = pl.BlockSpec((tm, d), lambda i: (i, 0))
    dtypes = (BF16, F32, F32, F32) if mode == "attn" else (F32, F32)
    return pl.pallas_call(
        functools.partial(_inproj_kernel, mode=mode, d=d, q_scale=_q_scale(d)),
        grid=(rows // tm,),
        in_specs=[row_spec,
                  pl.BlockSpec((1, d), lambda i: (0, 0)),
                  pl.BlockSpec((d, n_out), lambda i: (0, 0))],
        out_specs=[row_spec] * len(dtypes),
        out_shape=[jax.ShapeDtypeStruct((rows, d), t) for t in dtypes],
        compiler_params=_params("arbitrary"),
        name=f"inproj_{mode}",
    )(x, norm_g.reshape(1, d), w_bf16)


def _inproj_attn_prompt_kernel(x_ref, g_ref, wt_ref, wg_ref, *refs, d, q_scale):
    qt_ref, kt_ref, kb_ref, vt_ref, vtb_ref, gate_ref = refs[-6:]
    n = _rmsnorm(x_ref[0], g_ref[...]).astype(BF16)

    def chunk_t(c):
        return lax.dot_general(wt_ref[c * d:(c + 1) * d, :], n, _NT, preferred_element_type=F32)

    qt_ref[0, 0] = (chunk_t(0) * q_scale).astype(BF16)
    kt = chunk_t(1)
    kt_ref[0, 0] = kt
    kb_ref[0] = jnp.transpose(kt).astype(BF16)
    vt = chunk_t(2)
    vt_ref[0, 0] = vt
    vtb_ref[0, 0] = vt.astype(BF16)
    gate_ref[0] = jnp.dot(n, wg_ref[...], preferred_element_type=F32)


def _inproj_attn_prompt(x, norm_g, w_in, layer, n_layers, kv_all):
    b, s, d = x.shape
    blk = min(ATTN_BLOCK, s)
    assert s % blk == 0
    nt = s // blk
    wt = jnp.transpose(w_in[:, :3 * d]).astype(BF16)
    wg = w_in[:, 3 * d:].astype(BF16)
    row_spec = pl.BlockSpec((1, blk, d), lambda bi, t: (bi, t, 0))
    col_spec = pl.BlockSpec((1, 1, d, blk), lambda bi, t: (layer, bi, 0, t))
    tile_spec = pl.BlockSpec((1, 1, d, blk), lambda bi, t: (bi, t, 0, 0))
    const = lambda shape: pl.BlockSpec(shape, lambda bi, t: (0,) * len(shape))
    in_specs = [row_spec, const((1, d)), const((3 * d, d)), const((d, d))]
    operands = [x, norm_g.reshape(1, d), wt, wg]
    aliases = {}
    if kv_all is not None:
        in_specs += [pl.BlockSpec(memory_space=pl.ANY)] * 2
        aliases = {len(operands): 1, len(operands) + 1: 3}
        operands += list(kv_all)
    return pl.pallas_call(
        functools.partial(_inproj_attn_prompt_kernel, d=d, q_scale=_q_scale(d)),
        grid=(b, nt),
        in_specs=in_specs,
        out_specs=[tile_spec, col_spec, row_spec, col_spec, tile_spec, row_spec],
        out_shape=[jax.ShapeDtypeStruct((b, nt, d, blk), BF16),
                   jax.ShapeDtypeStruct((n_layers, b, d, s), F32),
                   jax.ShapeDtypeStruct((b, s, d), BF16),
                   jax.ShapeDtypeStruct((n_layers, b, d, s), F32),
                   jax.ShapeDtypeStruct((b, nt, d, blk), BF16),
                   jax.ShapeDtypeStruct((b, s, d), F32)],
        input_output_aliases=aliases,
        compiler_params=_params("arbitrary", "arbitrary"),
        name="inproj_attn_prompt",
    )(*operands)


def _gate_out(mix, gate, h, w_out_ref):
    act = (mix * _silu(gate)).astype(BF16)
    return h + jnp.dot(act, w_out_ref[...], preferred_element_type=F32)


def _outproj_kernel(o_ref, gate_ref, h_ref, w_ref, fg_ref, out_ref, *, final_norm):
    h_new = _gate_out(o_ref[...], gate_ref[...], h_ref[...], w_ref)
    if final_norm:
        h_new = _rmsnorm(h_new, fg_ref[...])
    out_ref[...] = h_new


def _outproj(o, gate, h, w_out_bf16, final_g, final_norm, tm_want):
    rows, d = h.shape
    tm = _row_tile(rows, tm_want)
    row_spec = pl.BlockSpec((tm, d), lambda i: (i, 0))
    return pl.pallas_call(
        functools.partial(_outproj_kernel, final_norm=final_norm),
        grid=(rows // tm,),
        in_specs=[row_spec, row_spec, row_spec,
                  pl.BlockSpec((d, d), lambda i: (0, 0)),
                  pl.BlockSpec((1, d), lambda i: (0, 0))],
        out_specs=row_spec,
        out_shape=jax.ShapeDtypeStruct((rows, d), F32),
        compiler_params=_params("arbitrary"),
        name="outproj_final" if final_norm else "outproj",
    )(o, gate, h, w_out_bf16, final_g.reshape(1, d))


def _attn_prompt_kernel(bias_ref, qt_ref, k_ref, vt_ref, o_ref, acc_ref, carry_ref, y_buf,
                        *, blk, n_heads, n_qblocks):
    hp = pl.program_id(1)
    grp = ATTN_GROUP
    half = LANES // HEADS_PER_BLOCK
    chan = lax.broadcasted_iota(jnp.int32, (LANES, 1), 0)
    first_head = chan < half

    col2 = lax.broadcasted_iota(jnp.int32, (1, 2 * blk), 1)

    def bias_piece(p):
        return jnp.where(col2 < blk, bias_ref[p * n_heads + 2 * hp], bias_ref[p * n_heads + 2 * hp + 1])

    bias_rows = jnp.where(chan == 0, bias_piece(0),
                          jnp.where(chan == 1, bias_piece(1),
                                    jnp.where(chan == 2, bias_piece(2), 0.0))).astype(BF16)
    lane = lax.broadcasted_iota(jnp.int32, (blk, LANES), 1)
    ones_cols = (lane < 3).astype(BF16)

    ss = lax.broadcasted_iota(jnp.int32, (blk, blk), 0)
    jj = lax.broadcasted_iota(jnp.int32, (blk, blk), 1)
    suffix_ones = (jj >= ss).astype(BF16)

    def per_query_block(j, _):
        qi = pl.program_id(2) * n_qblocks + j
        qt = qt_ref[0, j]
        zero = jnp.zeros_like(qt)
        qq = jnp.concatenate([jnp.where(first_head, qt, zero), jnp.where(first_head, zero, qt)], axis=1)
        qq_ext = jnp.concatenate([qq, bias_rows], axis=0)
        acc_ref[...] = jnp.zeros_like(acc_ref)
        carry_ref[...] = jnp.zeros_like(carry_ref)

        def tiles(kbs, diagonal):
            n = len(kbs)
            for i, kb in enumerate(kbs):
                ks = k_ref[0, pl.ds(pl.multiple_of(kb * blk, blk), blk), :]
                ks_ext = jnp.concatenate([ks, ones_cols], axis=1)
                y_buf[i] = jnp.dot(ks_ext, qq_ext, preferred_element_type=F32)
            if any(diagonal):
                r = lax.broadcasted_iota(jnp.int32, (blk, 2 * blk), 0)
                c = lax.broadcasted_iota(jnp.int32, (blk, 2 * blk), 1)
                visible = r < jnp.where(c < blk, c, c - blk)
            parts = []

            def second_half(i, c_local):
                a = jnp.exp2(y_buf[i] - c_local)
                if diagonal[i]:
                    a = jnp.where(visible, a, 0.0)
                ab = a.astype(BF16)
                vt = vt_ref[0, kbs[i]]
                p0 = jnp.dot(vt[:half], ab[:, :blk], preferred_element_type=F32)
                p1 = jnp.dot(vt[half:], ab[:, blk:], preferred_element_type=F32)
                parts.append((p0, p1, c_local[0:1, :]))

            pending = None
            for i in range(n):
                sp = _softplus2(y_buf[i])
                if diagonal[i]:
                    sp = jnp.where(visible, sp, 0.0)
                c_local = jnp.dot(suffix_ones, sp.astype(BF16), preferred_element_type=F32)
                if pending is not None:
                    second_half(*pending)
                pending = (i, c_local)
            second_half(*pending)
            return parts

        def accumulate(parts):
            for p0, p1, total in parts:
                carry = carry_ref[...]
                decay = jnp.exp2(-carry)
                acc_ref[0:half, :] += p0 * decay[:, :blk]
                acc_ref[half:, :] += p1 * decay[:, blk:]
                carry_ref[...] = carry + total

        for lead in range(grp):
            @pl.when(qi % grp == lead)
            def _(lead=lead):
                accumulate(tiles([qi - i for i in range(lead + 1)], [True] + [False] * lead))

        def group(g, _):
            top = qi - qi % grp - 1 - grp * g
            accumulate(tiles([top - i for i in range(grp)], [False] * grp))
            return 0

        lax.fori_loop(0, qi // grp, group, 0)

        o_ref[0, pl.ds(pl.multiple_of(j * blk, blk), blk), :] = jnp.transpose(acc_ref[...])
        return 0

    lax.fori_loop(0, n_qblocks, per_query_block, 0)


def _attn_prompt(qt, kb, vtb, bias2):
    b, nt, d, blk = qt.shape
    s = nt * blk
    n_pairs = d // LANES
    hi = bias2.astype(BF16).astype(F32)
    mid = (bias2 - hi).astype(BF16).astype(F32)
    lo = (bias2 - hi - mid).astype(BF16).astype(F32)
    pieces = jnp.concatenate([hi, mid, lo])
    nq = math.gcd(ATTN_QBLOCKS_PER_STEP, nt)
    return pl.pallas_call(
        functools.partial(_attn_prompt_kernel, blk=blk, n_heads=bias2.shape[0], n_qblocks=nq),
        grid_spec=pltpu.PrefetchScalarGridSpec(
            num_scalar_prefetch=1,
            grid=(b, n_pairs, nt // nq),
            in_specs=[pl.BlockSpec((1, nq, LANES, blk), lambda bi, hp, qs, bias: (bi, qs, hp, 0)),
                      pl.BlockSpec((1, s, LANES), lambda bi, hp, qs, bias: (bi, 0, hp)),
                      pl.BlockSpec((1, nt, LANES, blk), lambda bi, hp, qs, bias: (bi, 0, hp, 0))],
            out_specs=pl.BlockSpec((1, nq * blk, LANES), lambda bi, hp, qs, bias: (bi, qs, hp)),
            scratch_shapes=[pltpu.VMEM((LANES, blk), F32), pltpu.VMEM((1, 2 * blk), F32),
                            pltpu.VMEM((ATTN_GROUP, blk, 2 * blk), F32)]),
        out_shape=jax.ShapeDtypeStruct((b, s, d), F32),
        compiler_params=pltpu.CompilerParams(
            dimension_semantics=("arbitrary", "arbitrary", "arbitrary"), vmem_limit_bytes=VMEM_LIMIT_BYTES,
        ),
        name="attn_prompt",
    )(pieces, qt, kb, vtb)


def _attn_decode_kernel(pt_ref, q_ref, bias_ref, *refs, n_pages_step):
    k_refs = refs[:n_pages_step]
    v_refs = refs[n_pages_step:2 * n_pages_step]
    o_ref, qb_ref, acc_ref, carry_ref, y_ref, a_ref = refs[2 * n_pages_step:]
    step = pl.program_id(1)

    @pl.when(step == 0)
    def _():
        qb_ref[...] = jnp.broadcast_to(q_ref[0], qb_ref.shape)
        acc_ref[...] = jnp.zeros_like(acc_ref)
        carry_ref[...] = jnp.zeros_like(carry_ref)

    jj = lax.broadcasted_iota(jnp.int32, (PAGE_SIZE, PAGE_SIZE), 0)
    ss = lax.broadcasted_iota(jnp.int32, (PAGE_SIZE, PAGE_SIZE), 1)
    suffix_ones = (jj >= ss).astype(BF16)
    bias = bias_ref[...]
    carry = carry_ref[...]

    for p in range(n_pages_step):
        for h in range(N_HEADS):
            y_ref[h:h + 1, :] = jnp.sum(qb_ref[h] * k_refs[p][h], axis=0, keepdims=True)
        y = y_ref[...] + bias
        sp = _softplus2(y)
        c_local = jnp.dot(sp.astype(BF16), suffix_ones, preferred_element_type=F32)
        a_ref[p] = jnp.exp2(y - c_local - carry)
        carry = carry + c_local[:, 0:1]
    carry_ref[...] = carry

    for h in range(N_HEADS):
        acc = acc_ref[h]
        for p in range(n_pages_step):
            acc = acc + a_ref[p, h:h + 1, :] * v_refs[p][h]
        acc_ref[h] = acc

    @pl.when(step == pl.num_programs(1) - 1)
    def _():
        o_ref[0] = jnp.sum(acc_ref[...], axis=2, keepdims=True)


def _attn_decode(qs, cache_kt, cache_vt, layer, page_table, bias2):
    db, d = qs.shape
    dh = d // N_HEADS
    n_pages = page_table.shape[1]
    pps = min(DECODE_PAGES_PER_STEP, n_pages)
    assert n_pages % pps == 0

    def page_spec(p):
        def index(bi, st, pt):
            return (layer, pt[bi * n_pages + (n_pages - 1 - (st * pps + p))], 0, 0, 0)
        return pl.BlockSpec((None, None, N_HEADS, dh, PAGE_SIZE), index)

    specs = [page_spec(p) for p in range(pps)]
    out = pl.pallas_call(
        functools.partial(_attn_decode_kernel, n_pages_step=pps),
        grid_spec=pltpu.PrefetchScalarGridSpec(
            num_scalar_prefetch=1,
            grid=(db, n_pages // pps),
            in_specs=[pl.BlockSpec((1, N_HEADS, dh, 1), lambda bi, st, pt: (bi, 0, 0, 0)),
                      pl.BlockSpec((N_HEADS, 1), lambda bi, st, pt: (0, 0))] + specs + specs,
            out_specs=pl.BlockSpec((1, N_HEADS, dh, 1), lambda bi, st, pt: (bi, 0, 0, 0)),
            scratch_shapes=[pltpu.VMEM((N_HEADS, dh, PAGE_SIZE), F32),
                            pltpu.VMEM((N_HEADS, dh, PAGE_SIZE), F32),
                            pltpu.VMEM((N_HEADS, 1), F32),
                            pltpu.VMEM((N_HEADS, PAGE_SIZE), F32),
                            pltpu.VMEM((pps, N_HEADS, PAGE_SIZE), F32)]),
        out_shape=jax.ShapeDtypeStruct((db, N_HEADS, dh, 1), F32),
        compiler_params=_params("arbitrary", "arbitrary"),
        name="attn_decode",
    )(page_table.reshape(-1), qs.reshape(db, N_HEADS, dh, 1), bias2.reshape(N_HEADS, 1),
      *([cache_kt] * pps), *([cache_vt] * pps))
    return out.reshape(db, d)


def _pool_groups(window_sum, u_of, inv_cnt_of, wg_ref, scale):
    outs = []
    for gi, w in enumerate(POOL_WINDOWS):
        mixed = (window_sum(gi, w) * inv_cnt_of(w) - u_of(gi)).astype(BF16)
        outs.append(jnp.dot(mixed, wg_ref[gi], preferred_element_type=F32))
    return jnp.concatenate(outs, axis=-1) * scale


def _pool_prompt_kernel(u_ref, prev_ref, gate_ref, h_ref, wg_ref, scale_ref, w_ref, out_ref, ext_ref,
                        *, tm, grp):
    t = pl.program_id(1)
    prev = prev_ref[0]
    ext_ref[0:POOL_PREV_ROWS, :] = jnp.where(t > 0, prev, jnp.zeros_like(prev))
    ext_ref[POOL_PREV_ROWS:, :] = u_ref[0]
    pos = t * tm + lax.broadcasted_iota(jnp.int32, (tm, 1), 0)

    def window_sum(gi, w):
        sl = slice(gi * grp, (gi + 1) * grp)
        s = ext_ref[:, sl]
        k = 1
        while k < w:
            s = s + jnp.concatenate([s[:k], s[:-k]], axis=0)
            k *= 2
        return s[POOL_PREV_ROWS:POOL_PREV_ROWS + tm]

    def u_of(gi):
        return ext_ref[POOL_PREV_ROWS:POOL_PREV_ROWS + tm, gi * grp:(gi + 1) * grp]

    def inv_cnt_of(w):
        return 1.0 / jnp.minimum(w, pos + 1).astype(F32)

    mix = _pool_groups(window_sum, u_of, inv_cnt_of, wg_ref, scale_ref[...])
    out_ref[0] = _gate_out(mix, gate_ref[0], h_ref[0], w_ref)


def _pool_prompt(u, gate, h, wg_bf16, scale, w_out_bf16, tm_want):
    b, s, d = u.shape
    tm = _row_tile(s, tm_want)
    assert tm % POOL_PREV_ROWS == 0
    grp = d // len(POOL_WINDOWS)
    ratio = tm // POOL_PREV_ROWS
    row_spec = pl.BlockSpec((1, tm, d), lambda bi, t: (bi, t, 0))
    return pl.pallas_call(
        functools.partial(_pool_prompt_kernel, tm=tm, grp=grp),
        grid=(b, s // tm),
        in_specs=[row_spec,
                  pl.BlockSpec((1, POOL_PREV_ROWS, d), lambda bi, t: (bi, jnp.maximum(t * ratio - 1, 0), 0)),
                  row_spec, row_spec,
                  pl.BlockSpec((len(POOL_WINDOWS), grp, grp), lambda bi, t: (0, 0, 0)),
                  pl.BlockSpec((1, d), lambda bi, t: (0, 0)),
                  pl.BlockSpec((d, d), lambda bi, t: (0, 0))],
        out_specs=row_spec,
        out_shape=jax.ShapeDtypeStruct((b, s, d), F32),
        scratch_shapes=[pltpu.VMEM((tm + POOL_PREV_ROWS, d), F32)],
        compiler_params=_params("arbitrary", "arbitrary"),
        name="pool_prompt",
    )(u, u, gate, h, wg_bf16, scale.reshape(1, d), w_out_bf16)


def _pool_sample_kernel(hist_ref, u_ref, gate_ref, h_ref, wg_ref, scale_ref, w_ref, out_ref, *, grp, past):
    n_hist = hist_ref.shape[0]

    def window_sum(gi, w):
        sl = slice(gi * grp, (gi + 1) * grp)
        s = u_ref[:, sl]
        for i in range(1, w):
            s = s + hist_ref[n_hist - i, :, sl]
        return s

    def u_of(gi):
        return u_ref[:, gi * grp:(gi + 1) * grp]

    def inv_cnt_of(w):
        return 1.0 / float(min(w, past + 1))

    mix = _pool_groups(window_sum, u_of, inv_cnt_of, wg_ref, scale_ref[...])
    out_ref[...] = _gate_out(mix, gate_ref[...], h_ref[...], w_ref)


def _pool_sample(hist_t, u, gate, h, wg_bf16, scale, w_out_bf16, past):
    db, d = u.shape
    grp = d // len(POOL_WINDOWS)
    return pl.pallas_call(
        functools.partial(_pool_sample_kernel, grp=grp, past=past),
        out_shape=jax.ShapeDtypeStruct((db, d), F32),
        compiler_params=pltpu.CompilerParams(vmem_limit_bytes=VMEM_LIMIT_BYTES),
        name="pool_sample",
    )(hist_t, u, gate, h, wg_bf16, scale.reshape(1, d), w_out_bf16)


def _ln_silu(y, ln_g, ln_b):
    mu = jnp.mean(y, axis=-1, keepdims=True)
    yc = y - mu
    yn = yc * lax.rsqrt(jnp.mean(yc * yc, axis=-1, keepdims=True) + LN_EPS) * ln_g + ln_b
    return _silu(yn)


def _conv_prompt_kernel(glu_ref, prev_ref, gate_ref, h_ref, dw_ref, db_ref, lg_ref, lb_ref, w_ref,
                        out_ref, ext_ref, shifted_ref, mix_ref, *, tm):
    t = pl.program_id(1)
    prev = prev_ref[0]
    ext_ref[0:CONV_PREV_ROWS, :] = jnp.where(t > 0, prev, jnp.zeros_like(prev))
    ext_ref[CONV_PREV_ROWS:, :] = glu_ref[0]
    lead = CONV_PREV_ROWS - (CONV_WIDTH - 1)
    n_shifted = shifted_ref.shape[1]
    for shift in range(1, SUBLANES):
        shifted_ref[shift - 1] = ext_ref[shift:shift + n_shifted, :]
    for c in range(tm // CONV_ROW_CHUNK):
        r0 = c * CONV_ROW_CHUNK
        y = None
        for i in range(CONV_WIDTH):
            shift = (lead + i) % SUBLANES
            q = r0 + lead + i - shift
            if shift == 0:
                rows = ext_ref[q:q + CONV_ROW_CHUNK, :]
            else:
                rows = shifted_ref[shift - 1, q:q + CONV_ROW_CHUNK, :]
            term = dw_ref[i:i + 1, :] * rows
            y = term if y is None else y + term
        mix_ref[r0:r0 + CONV_ROW_CHUNK, :] = _ln_silu(y + db_ref[...], lg_ref[...], lb_ref[...])
    out_ref[0] = _gate_out(mix_ref[...], gate_ref[0], h_ref[0], w_ref)


def _conv_prompt(glu, gate, h, dw_w, dw_b, ln_g, ln_b, w_out_bf16, tm_want):
    b, s, d = glu.shape
    tm = _row_tile(s, tm_want)
    assert tm % CONV_PREV_ROWS == 0 and tm % CONV_ROW_CHUNK == 0
    ratio = tm // CONV_PREV_ROWS
    row_spec = pl.BlockSpec((1, tm, d), lambda bi, t: (bi, t, 0))
    vec_spec = pl.BlockSpec((1, d), lambda bi, t: (0, 0))
    return pl.pallas_call(
        functools.partial(_conv_prompt_kernel, tm=tm),
        grid=(b, s // tm),
        in_specs=[row_spec,
                  pl.BlockSpec((1, CONV_PREV_ROWS, d), lambda bi, t: (bi, jnp.maximum(t * ratio - 1, 0), 0)),
                  row_spec, row_spec,
                  pl.BlockSpec((CONV_WIDTH, d), lambda bi, t: (0, 0)),
                  vec_spec, vec_spec, vec_spec,
                  pl.BlockSpec((d, d), lambda bi, t: (0, 0))],
        out_specs=row_spec,
        out_shape=jax.ShapeDtypeStruct((b, s, d), F32),
        scratch_shapes=[pltpu.VMEM((tm + CONV_PREV_ROWS, d), F32),
                        pltpu.VMEM((SUBLANES - 1, tm + CONV_PREV_ROWS - SUBLANES, d), F32),
                        pltpu.VMEM((tm, d), F32)],
        compiler_params=_params("arbitrary", "arbitrary"),
        name="conv_prompt",
    )(glu, glu, gate, h, dw_w, dw_b.reshape(1, d), ln_g.reshape(1, d), ln_b.reshape(1, d), w_out_bf16)


def _conv_sample_kernel(hist_ref, glu_ref, gate_ref, h_ref, dw_ref, db_ref, lg_ref, lb_ref, w_ref, out_ref):
    n_hist = hist_ref.shape[0]
    y = dw_ref[n_hist:n_hist + 1, :] * glu_ref[...]
    for i in range(n_hist):
        y = y + dw_ref[i:i + 1, :] * hist_ref[i]
    mix = _ln_silu(y + db_ref[...], lg_ref[...], lb_ref[...])
    out_ref[...] = _gate_out(mix, gate_ref[...], h_ref[...], w_ref)


def _conv_sample(hist_t, glu, gate, h, dw_w, dw_b, ln_g, ln_b, w_out_bf16):
    db, d = glu.shape
    return pl.pallas_call(
        _conv_sample_kernel,
        out_shape=jax.ShapeDtypeStruct((db, d), F32),
        compiler_params=pltpu.CompilerParams(vmem_limit_bytes=VMEM_LIMIT_BYTES),
        name="conv_sample",
    )(hist_t, glu, gate, h, dw_w, dw_b.reshape(1, d), ln_g.reshape(1, d), ln_b.reshape(1, d), w_out_bf16)


def kernel(x_prompt, x_sample, cache_k, cache_v, state_pool, state_conv, page_table, norm_g, final_norm_g,
           attn_w_in, attn_bias, attn_w_out, pool_w_in, pool_w_group, pool_scale, pool_w_out, conv_w_in,
           conv_dw_w, conv_dw_b, conv_ln_g, conv_ln_b, conv_w_out):
    b, s, d = x_prompt.shape
    db, ds, _ = x_sample.shape
    assert ds == 1 and d % LANES == 0 and d // N_HEADS * HEADS_PER_BLOCK == LANES
    depth = norm_g.shape[0]
    past = page_table.shape[1] * PAGE_SIZE
    dh = d // N_HEADS
    pool_hist = state_pool.shape[2]
    conv_hist = state_conv.shape[2]
    assert pool_hist == max(POOL_WINDOWS) - 1 and conv_hist == CONV_WIDTH - 1

    cache_kt = jnp.transpose(cache_k, (0, 1, 3, 4, 2))
    cache_vt = jnp.transpose(cache_v, (0, 1, 3, 4, 2))

    n_attn = len(range(0, depth, N_MIXERS))

    def heads_last(xt):
        return jnp.transpose(xt.reshape(n_attn, b, N_HEADS, dh, s), (0, 1, 4, 2, 3))

    hp = x_prompt
    hs = x_sample.reshape(db, d)
    kv_all = None
    ks_l, vs_l = [], []
    poolp_l, pools_l, convp_l, convs_l = [], [], [], []
    for i in range(depth):
        kind = i % N_MIXERS
        j = i // N_MIXERS
        last = i == depth - 1
        if kind == 0:
            w_out = attn_w_out[j].astype(BF16)
            bias2 = attn_bias[j] * LOG2E
            qt, kt_all, kb, vt_all, vtb, gp = _inproj_attn_prompt(hp, norm_g[i], attn_w_in[j], j, n_attn, kv_all)
            kv_all = (kt_all, vt_all)
            qsb, ks_, vs, gs = _inproj(hs, norm_g[i], attn_w_in[j].astype(BF16), "attn", 256)
            op = _attn_prompt(qt, kb, vtb, bias2)
            os_ = _attn_decode(qsb.astype(F32), cache_kt, cache_vt, j, page_table, bias2)
            hp = _outproj(op.reshape(b * s, d), gp.reshape(b * s, d), hp.reshape(b * s, d), w_out,
                          final_norm_g, last, 512).reshape(b, s, d)
            hs = _outproj(os_, gs, hs, w_out, final_norm_g, last, 512)
            ks_l.append(ks_.reshape(db, ds, N_HEADS, dh))
            vs_l.append(vs.reshape(db, ds, N_HEADS, dh))
        elif kind == 1:
            w_in = pool_w_in[j].astype(BF16)
            w_out = pool_w_out[j].astype(BF16)
            wg = pool_w_group[j].astype(BF16)
            up, gp = _inproj(hp.reshape(b * s, d), norm_g[i], w_in, "pool", 512)
            us, gs = _inproj(hs, norm_g[i], w_in, "pool", 512)
            up3 = up.reshape(b, s, d)
            hp = _pool_prompt(up3, gp.reshape(b, s, d), hp, wg, pool_scale[j], w_out, 256)
            hist_t = jnp.transpose(state_pool[j], (1, 0, 2))
            hs = _pool_sample(hist_t, us, gs, hs, wg, pool_scale[j], w_out, past)
            if s >= pool_hist:
                poolp_l.append(up3[:, s - pool_hist:])
            else:
                poolp_l.append(jnp.concatenate([jnp.zeros((b, pool_hist - s, d), F32), up3], axis=1))
            pools_l.append(jnp.concatenate([state_pool[j][:, ds:], us.reshape(db, ds, d)], axis=1))
        else:
            w_in = conv_w_in[j].astype(BF16)
            w_out = conv_w_out[j].astype(BF16)
            glu_p, gp = _inproj(hp.reshape(b * s, d), norm_g[i], w_in, "conv", 512)
            glu_s, gs = _inproj(hs, norm_g[i], w_in, "conv", 512)
            glu3 = glu_p.reshape(b, s, d)
            hp = _conv_prompt(glu3, gp.reshape(b, s, d), hp, conv_dw_w[j], conv_dw_b[j],
                              conv_ln_g[j], conv_ln_b[j], w_out, 256)
            hist_t = jnp.transpose(state_conv[j], (1, 0, 2))
            hs = _conv_sample(hist_t, glu_s, gs, hs, conv_dw_w[j], conv_dw_b[j], conv_ln_g[j], conv_ln_b[j],
                              w_out)
            if s >= conv_hist:
                convp_l.append(glu3[:, s - conv_hist:])
            else:
                convp_l.append(jnp.concatenate([jnp.zeros((b, conv_hist - s, d), F32), glu3], axis=1))
            convs_l.append(jnp.concatenate([state_conv[j][:, ds:], glu_s.reshape(db, ds, d)], axis=1))
    assert depth % N_MIXERS == 1, "the final RMSNorm is fused into the attention output projection"
    return (hp, hs.reshape(db, ds, d), heads_last(kv_all[0]), heads_last(kv_all[1]), jnp.stack(ks_l),
            jnp.stack(vs_l), jnp.stack(poolp_l), jnp.stack(pools_l), jnp.stack(convp_l), jnp.stack(convs_l))
```
